```python
import jax, jax.numpy as jnp
from jax import lax
import numpy as np


D_MODEL = 4096
BATCH = 4
SEQ = 2048
DEPTH = 4

D_FF = 5632
MIX_WIDTH = D_MODEL
POOL_GROUPS = 4
POOL_GROUP_DIM = MIX_WIDTH // 16
POOL_WIDTH = POOL_GROUPS * POOL_GROUP_DIM
POOL_WINDOWS = (2, 4, 8, 16)
CONV_WIDTH = (MIX_WIDTH - POOL_WIDTH) // 2
CONV_K = 3
SGU_HEAD_DIM = 128
SGU_WIDTH = MIX_WIDTH - POOL_WIDTH - CONV_WIDTH
SGU_HEADS = SGU_WIDTH // SGU_HEAD_DIM
SGU_CHUNK = 128
PROJ_WIDTH = POOL_WIDTH + 3 * CONV_WIDTH + 2 * SGU_WIDTH
PROJ_SPLITS = (POOL_WIDTH,
               POOL_WIDTH + CONV_WIDTH,
               POOL_WIDTH + 2 * CONV_WIDTH,
               POOL_WIDTH + 3 * CONV_WIDTH,
               POOL_WIDTH + 3 * CONV_WIDTH + SGU_WIDTH)
NORM_EPS = 1e-6
LN_EPS = 1e-5

kernel_name = 'hybrid_pool_conv_sgu_macaron'


def rms_norm(x, g):
    xf = x.astype(jnp.float32)
    y = xf * lax.rsqrt(jnp.mean(xf * xf, axis=-1, keepdims=True) + NORM_EPS)
    return (y * g.astype(jnp.float32)).astype(x.dtype)


def swiglu(h, w_gate, w_up, w_down):
    return (jax.nn.silu(h @ w_gate) * (h @ w_up)) @ w_down


def causal_multiscale_pool(a):
    b, t, _ = a.shape
    a4 = a.reshape(b, t, POOL_GROUPS, POOL_GROUP_DIM).astype(jnp.float32)
    cs = jnp.cumsum(a4, axis=1)
    cs = jnp.concatenate([jnp.zeros_like(cs[:, :1]), cs], axis=1)
    pos = jnp.arange(t)
    windows = jnp.array(POOL_WINDOWS, dtype=jnp.int32)
    start = jnp.maximum(pos[:, None] + 1 - windows[None, :], 0)
    lower = cs[:, start, jnp.arange(POOL_GROUPS)[None, :]]
    count = (pos[:, None] + 1 - start).astype(jnp.float32)
    pooled = (cs[:, 1:] - lower) / count[None, :, :, None]
    return (pooled - a4).astype(a.dtype)


def causal_depthwise_conv(z, w):
    return lax.conv_general_dilated(
        z, w[:, None, :], window_strides=(1,), padding=[(CONV_K - 1, 0)],
        dimension_numbers=('NWC', 'WIO', 'NWC'), feature_group_count=z.shape[-1])


def chunked_spatial_gating(u, v, ln_g, w_s, b_s):
    b, t, _ = u.shape
    n = t // SGU_CHUNK
    vh = v.reshape(b, n, SGU_CHUNK, SGU_HEADS, SGU_HEAD_DIM).astype(jnp.float32)
    mu = jnp.mean(vh, axis=-1, keepdims=True)
    var = jnp.mean(jnp.square(vh - mu), axis=-1, keepdims=True)
    vh = (vh - mu) * lax.rsqrt(var + LN_EPS) * ln_g.reshape(SGU_HEADS, SGU_HEAD_DIM).astype(jnp.float32)
    vh = vh.astype(u.dtype)
    mask = jnp.tril(jnp.ones((SGU_CHUNK, SGU_CHUNK), dtype=bool))
    ws = jnp.where(mask[None], w_s, 0)
    mixed = jnp.einsum('hts,bnshd->bnthd', ws, vh) + b_s.T[None, None, :, :, None]
    return u * mixed.reshape(b, t, SGU_WIDTH)


def setup_inputs(seed: int = 0) -> dict:
    key = jax.random.key(seed)
    ks = jax.random.split(key, 24)
    f32 = jnp.float32
    L, D, F = DEPTH, D_MODEL, D_FF

    def nrm(k, shape, scale):
        return jax.random.normal(k, shape, f32) * scale

    def gain(k, shape):
        return 1.0 + 0.1 * jax.random.normal(k, shape, f32)

    return {
        'x': jax.random.normal(ks[0], (BATCH, SEQ, D), f32),
        'ffn1_norm': gain(ks[1], (L, D)),
        'ffn1_gate': nrm(ks[2], (L, D, F), D ** -0.5),
        'ffn1_up': nrm(ks[3], (L, D, F), D ** -0.5),
        'ffn1_down': nrm(ks[4], (L, F, D), F ** -0.5),
        'mix_norm': gain(ks[5], (L, D)),
        'w_in': nrm(ks[6], (L, D, PROJ_WIDTH), D ** -0.5),
        'pool_w': nrm(ks[7], (L, POOL_GROUPS, POOL_GROUP_DIM, POOL_GROUP_DIM), POOL_GROUP_DIM ** -0.5),
        'pool_scale': gain(ks[8], (L, POOL_WIDTH)),
        'conv_w': nrm(ks[9], (L, CONV_K, CONV_WIDTH), CONV_K ** -0.5),
        'sgu_norm': gain(ks[10], (L, SGU_WIDTH)),
        'sgu_w': nrm(ks[11], (L, SGU_HEADS, SGU_CHUNK, SGU_CHUNK), SGU_CHUNK ** -0.5),
        'sgu_b': gain(ks[12], (L, SGU_HEADS, SGU_CHUNK)),
        'w_out': nrm(ks[13], (L, MIX_WIDTH, D), MIX_WIDTH ** -0.5),
        'ffn2_norm': gain(ks[14], (L, D)),
        'ffn2_gate': nrm(ks[15], (L, D, F), D ** -0.5),
        'ffn2_up': nrm(ks[16], (L, D, F), D ** -0.5),
        'ffn2_down': nrm(ks[17], (L, F, D), F ** -0.5),
        'final_norm': gain(ks[18], (D,)),
    }


def reference(x, ffn1_norm, ffn1_gate, ffn1_up, ffn1_down, mix_norm, w_in, pool_w, pool_scale,
              conv_w, sgu_norm, sgu_w, sgu_b, w_out, ffn2_norm, ffn2_gate, ffn2_up, ffn2_down,
              final_norm):
    b, t, _ = x.shape
    for l in range(DEPTH):
        h = rms_norm(x, ffn1_norm[l])
        x = x + 0.5 * swiglu(h, ffn1_gate[l], ffn1_up[l], ffn1_down[l])

        h = rms_norm(x, mix_norm[l])
        proj = h @ w_in[l]
        a, gb, gc, xc, u, v = jnp.split(proj, list(PROJ_SPLITS), axis=-1)

        pa = causal_multiscale_pool(a)
        ya = jnp.einsum('btgc,gcd->btgd', pa, pool_w[l]).reshape(b, t, POOL_WIDTH) * pool_scale[l]

        yb = gb * causal_depthwise_conv(gc * xc, conv_w[l])

        yc = chunked_spatial_gating(jax.nn.gelu(u), jax.nn.gelu(v), sgu_norm[l], sgu_w[l], sgu_b[l])

        x = x + jnp.concatenate([ya, yb, yc], axis=-1) @ w_out[l]

        h = rms_norm(x, ffn2_norm[l])
        x = x + 0.5 * swiglu(h, ffn2_gate[l], ffn2_up[l], ffn2_down[l])
    return rms_norm(x, final_norm)
```

```python
import functools

import jax
import jax.numpy as jnp
from jax import lax
from jax.experimental import pallas as pl
from jax.experimental.pallas import tpu as pltpu

D_MODEL = 4096
D_FF = 5632
DEPTH = 4
POOL_GROUPS = 4
POOL_GROUP_DIM = 256
POOL_WIDTH = POOL_GROUPS * POOL_GROUP_DIM
POOL_WINDOWS = (2, 4, 8, 16)
CONV_WIDTH = 1536
CONV_K = 3
SGU_HEAD_DIM = 128
SGU_WIDTH = 1536
SGU_CHUNK = 128
PROJ_WIDTH = POOL_WIDTH + 3 * CONV_WIDTH + 2 * SGU_WIDTH
NORM_EPS = 1e-6
LN_EPS = 1e-5

F32 = jnp.float32
BF16 = jnp.bfloat16

VMEM_LIMIT_BYTES = 56 * 1024 * 1024
NORM_ROWS = 256
MM_ROWS = 1024
MM_COLS = 512
MIX_COLS = 256
MIX_TIME = 256
POOL_HALO = 16
CONV_HALO = 8

N_POOL_BLOCKS = POOL_WIDTH // MIX_COLS
N_CONV_BLOCKS = CONV_WIDTH // MIX_COLS
N_SGU_BLOCKS = SGU_WIDTH // MIX_COLS
N_MIX_BLOCKS = N_POOL_BLOCKS + N_CONV_BLOCKS + N_SGU_BLOCKS
SGU_HEADS_PER_BLOCK = MIX_COLS // SGU_HEAD_DIM


def _params(*semantics):
    return pltpu.CompilerParams(dimension_semantics=semantics,
                                vmem_limit_bytes=VMEM_LIMIT_BYTES)


def _rmsnorm_kernel(x_ref, g_ref, o_ref):
    x = x_ref[...]
    ms = jnp.mean(x * x, axis=-1, keepdims=True)
    o_ref[...] = (x * lax.rsqrt(ms + NORM_EPS) * g_ref[...]).astype(o_ref.dtype)


def _rmsnorm(x, g, out_dtype):
    m, d = x.shape
    return pl.pallas_call(
        _rmsnorm_kernel,
        grid=(m // NORM_ROWS,),
        in_specs=[pl.BlockSpec((NORM_ROWS, d), lambda i: (i, 0)),
                  pl.BlockSpec((1, d), lambda i: (0, 0))],
        out_specs=pl.BlockSpec((NORM_ROWS, d), lambda i: (i, 0)),
        out_shape=jax.ShapeDtypeStruct((m, d), out_dtype),
        compiler_params=_params("parallel"),
        name="rmsnorm",
    )(x, g.reshape(1, d))


def _ffn_up_kernel(h_ref, wg_ref, wu_ref, o_ref):
    h = h_ref[...]
    g = jnp.dot(h, wg_ref[...], preferred_element_type=F32)
    u = jnp.dot(h, wu_ref[...], preferred_element_type=F32)
    o_ref[...] = (g * jax.nn.sigmoid(g) * u).astype(o_ref.dtype)


def _ffn_up(h, wg, wu):
    m, d = h.shape
    f = wg.shape[1]
    return pl.pallas_call(
        _ffn_up_kernel,
        grid=(m // MM_ROWS, f // MM_COLS),
        in_specs=[pl.BlockSpec((MM_ROWS, d), lambda i, j: (i, 0)),
                  pl.BlockSpec((d, MM_COLS), lambda i, j: (0, j)),
                  pl.BlockSpec((d, MM_COLS), lambda i, j: (0, j))],
        out_specs=pl.BlockSpec((MM_ROWS, MM_COLS), lambda i, j: (i, j)),
        out_shape=jax.ShapeDtypeStruct((m, f), BF16),
        compiler_params=_params("parallel", "parallel"),
        name="ffn_up",
    )(h, wg, wu)


def _residual_matmul_kernel(a_ref, w_ref, x_ref, o_ref, *, scale):
    acc = jnp.dot(a_ref[...], w_ref[...], preferred_element_type=F32)
    o_ref[...] = x_ref[...] + scale * acc


def _residual_matmul(a, w, x, scale):
    m, k = a.shape
    n = w.shape[1]
    return pl.pallas_call(
        functools.partial(_residual_matmul_kernel, scale=scale),
        grid=(m // MM_ROWS, n // MM_COLS),
        in_specs=[pl.BlockSpec((MM_ROWS, k), lambda i, j: (i, 0)),
                  pl.BlockSpec((k, MM_COLS), lambda i, j: (0, j)),
                  pl.BlockSpec((MM_ROWS, MM_COLS), lambda i, j: (i, j))],
        out_specs=pl.BlockSpec((MM_ROWS, MM_COLS), lambda i, j: (i, j)),
        out_shape=jax.ShapeDtypeStruct((m, n), F32),
        compiler_params=_params("parallel", "parallel"),
        name="residual_matmul",
    )(a, w, x)


def _proj_kernel(h_ref, w_ref, o_ref):
    o_ref[...] = jnp.dot(h_ref[...], w_ref[...], preferred_element_type=F32)


def _proj(h, w):
    m, d = h.shape
    n = w.shape[1]
    return pl.pallas_call(
        _proj_kernel,
        grid=(m // MM_ROWS, n // MM_COLS),
        in_specs=[pl.BlockSpec((MM_ROWS, d), lambda i, j: (i, 0)),
                  pl.BlockSpec((d, MM_COLS), lambda i, j: (0, j))],
        out_specs=pl.BlockSpec((MM_ROWS, MM_COLS), lambda i, j: (i, j)),
        out_shape=jax.ShapeDtypeStruct((m, n), F32),
        compiler_params=_params("parallel", "parallel"),
        name="proj",
    )(h, w)


def _gelu_tanh(x):
    return 0.5 * x * (1.0 + jnp.tanh(0.7978845608028654 * (x + 0.044715 * (x * x * x))))


def _time_chunks(ref, halo, n_time, chunk_fn):
    cols = ref.shape[-1]
    first = jnp.concatenate([jnp.zeros((halo, cols), F32), ref[0, 0:MIX_TIME, :]], axis=0)
    chunk_fn(0, first)

    def body(c, carry):
        t0 = pl.multiple_of(c * MIX_TIME, MIX_TIME)
        start = pl.multiple_of(t0 - halo, halo)
        chunk_fn(t0, ref[0, pl.ds(start, MIX_TIME + halo), :])
        return carry

    lax.fori_loop(1, n_time // MIX_TIME, body, 0)


def _pool_block(a_ref, pw_ref, ps_ref, o_ref, window):
    n_time = a_ref.shape[1]
    w = pw_ref[0]
    scale = ps_ref[...]

    def chunk(t0, rows):
        s = rows
        shift = 1
        while shift < window:
            s = s + pltpu.roll(s, shift, axis=0)
            shift *= 2
        t = t0 + lax.broadcasted_iota(jnp.int32, (MIX_TIME, 1), 0)
        count = jnp.minimum(t + 1, window).astype(F32)
        pa = s[POOL_HALO:] / count - rows[POOL_HALO:]
        y = jnp.dot(pa.astype(BF16), w, preferred_element_type=F32) * scale
        o_ref[0, pl.ds(t0, MIX_TIME), :] = y.astype(o_ref.dtype)

    _time_chunks(a_ref, POOL_HALO, n_time, chunk)


def _conv_block(gb_ref, gc_ref, xc_ref, cw_ref, o_ref):
    n_time = gb_ref.shape[1]
    w0 = cw_ref[0:1, :]
    w1 = cw_ref[1:2, :]
    w2 = cw_ref[2:3, :]
    cols = gb_ref.shape[-1]

    def chunk_rows(ref, t0):
        start = pl.multiple_of(t0 - CONV_HALO, CONV_HALO)
        return ref[0, pl.ds(start, MIX_TIME + CONV_HALO), :]

    def compute(t0, z):
        conv = w2 * z + w1 * pltpu.roll(z, 1, axis=0) + w0 * pltpu.roll(z, 2, axis=0)
        y = gb_ref[0, pl.ds(t0, MIX_TIME), :] * conv[CONV_HALO:]
        o_ref[0, pl.ds(t0, MIX_TIME), :] = y.astype(o_ref.dtype)

    z_first = gc_ref[0, 0:MIX_TIME, :] * xc_ref[0, 0:MIX_TIME, :]
    compute(0, jnp.concatenate([jnp.zeros((CONV_HALO, cols), F32), z_first], axis=0))

    def body(c, carry):
        t0 = pl.multiple_of(c * MIX_TIME, MIX_TIME)
        compute(t0, chunk_rows(gc_ref, t0) * chunk_rows(xc_ref, t0))
        return carry

    lax.fori_loop(1, n_time // MIX_TIME, body, 0)


def _sgu_block(u_ref, v_ref, lg_ref, ws_ref, bs_ref, o_ref):
    n_time = u_ref.shape[1]
    row = lax.broadcasted_iota(jnp.int32, (SGU_CHUNK, SGU_CHUNK), 0)
    col = lax.broadcasted_iota(jnp.int32, (SGU_CHUNK, SGU_CHUNK), 1)
    causal = row >= col
    for hh in range(SGU_HEADS_PER_BLOCK):
        lanes = slice(hh * SGU_HEAD_DIM, (hh + 1) * SGU_HEAD_DIM)
        ws = jnp.where(causal, ws_ref[hh], 0.0).astype(BF16)
        bias = bs_ref[hh]
        gain = lg_ref[:, lanes]

        def body(n, carry, lanes=lanes, ws=ws, bias=bias, gain=gain):
            t0 = pl.multiple_of(n * SGU_CHUNK, SGU_CHUNK)
            v = _gelu_tanh(v_ref[0, pl.ds(t0, SGU_CHUNK), lanes])
            mu = jnp.mean(v, axis=-1, keepdims=True)
            vc = v - mu
            var = jnp.mean(vc * vc, axis=-1, keepdims=True)
            vn = vc * lax.rsqrt(var + LN_EPS) * gain
            mixed = jnp.dot(ws, vn.astype(BF16), preferred_element_type=F32) + bias
            u = _gelu_tanh(u_ref[0, pl.ds(t0, SGU_CHUNK), lanes])
            o_ref[0, pl.ds(t0, SGU_CHUNK), lanes] = (u * mixed).astype(o_ref.dtype)
            return carry

        lax.fori_loop(0, n_time // SGU_CHUNK, body, 0)


def _mixer_kernel(a_ref, gb_ref, gc_ref, xc_ref, u_ref, v_ref,
                  pw_ref, ps_ref, cw_ref, lg_ref, ws_ref, bs_ref, o_ref):
    j = pl.program_id(1)
    for g, window in enumerate(POOL_WINDOWS):
        @pl.when(j == g)
        def _(window=window):
            _pool_block(a_ref, pw_ref, ps_ref, o_ref, window)

    @pl.when(jnp.logical_and(j >= N_POOL_BLOCKS, j < N_POOL_BLOCKS + N_CONV_BLOCKS))
    def _():
        _conv_block(gb_ref, gc_ref, xc_ref, cw_ref, o_ref)

    @pl.when(j >= N_POOL_BLOCKS + N_CONV_BLOCKS)
    def _():
        _sgu_block(u_ref, v_ref, lg_ref, ws_ref, bs_ref, o_ref)


def _clamp(j, lo, n):
    return jnp.clip(j - lo, 0, n - 1)


def _mixers(proj, pool_w, pool_scale, conv_w, sgu_norm, sgu_w, sgu_b):
    b, t, _ = proj.shape
    conv_lo = N_POOL_BLOCKS
    sgu_lo = N_POOL_BLOCKS + N_CONV_BLOCKS
    off_gb = POOL_WIDTH // MIX_COLS
    off_gc = off_gb + N_CONV_BLOCKS
    off_xc = off_gc + N_CONV_BLOCKS
    off_u = off_xc + N_CONV_BLOCKS
    off_v = off_u + N_SGU_BLOCKS

    def proj_spec(offset, lo, n):
        return pl.BlockSpec((1, t, MIX_COLS), lambda bi, j: (bi, 0, offset + _clamp(j, lo, n)))

    in_specs = [
        proj_spec(0, 0, N_POOL_BLOCKS),
        proj_spec(off_gb, conv_lo, N_CONV_BLOCKS),
        proj_spec(off_gc, conv_lo, N_CONV_BLOCKS),
        proj_spec(off_xc, conv_lo, N_CONV_BLOCKS),
        proj_spec(off_u, sgu_lo, N_SGU_BLOCKS),
        proj_spec(off_v, sgu_lo, N_SGU_BLOCKS),
        pl.BlockSpec((1, POOL_GROUP_DIM, POOL_GROUP_DIM),
                     lambda bi, j: (_clamp(j, 0, N_POOL_BLOCKS), 0, 0)),
        pl.BlockSpec((1, MIX_COLS), lambda bi, j: (0, _clamp(j, 0, N_POOL_BLOCKS))),
        pl.BlockSpec((CONV_K, MIX_COLS), lambda bi, j: (0, _clamp(j, conv_lo, N_CONV_BLOCKS))),
        pl.BlockSpec((1, MIX_COLS), lambda bi, j: (0, _clamp(j, sgu_lo, N_SGU_BLOCKS))),
        pl.BlockSpec((SGU_HEADS_PER_BLOCK, SGU_CHUNK, SGU_CHUNK),
                     lambda bi, j: (_clamp(j, sgu_lo, N_SGU_BLOCKS), 0, 0)),
        pl.BlockSpec((SGU_HEADS_PER_BLOCK, SGU_CHUNK, 1),
                     lambda bi, j: (_clamp(j, sgu_lo, N_SGU_BLOCKS), 0, 0)),
    ]
    return pl.pallas_call(
        _mixer_kernel,
        grid=(b, N_MIX_BLOCKS),
        in_specs=in_specs,
        out_specs=pl.BlockSpec((1, t, MIX_COLS), lambda bi, j: (bi, 0, j)),
        out_shape=jax.ShapeDtypeStruct((b, t, D_MODEL), BF16),
        compiler_params=_params("parallel", "arbitrary"),
        name="mixers",
    )(proj, proj, proj, proj, proj, proj,
      pool_w.astype(BF16), pool_scale.reshape(1, POOL_WIDTH), conv_w,
      sgu_norm.reshape(1, SGU_WIDTH), sgu_w, sgu_b[..., None])


def _ffn(x, norm_g, wg, wu, wd):
    h = _rmsnorm(x, norm_g, BF16)
    act = _ffn_up(h, wg.astype(BF16), wu.astype(BF16))
    return _residual_matmul(act, wd.astype(BF16), x, 0.5)


def kernel(x, ffn1_norm, ffn1_gate, ffn1_up, ffn1_down, mix_norm, w_in, pool_w, pool_scale,
           conv_w, sgu_norm, sgu_w, sgu_b, w_out, ffn2_norm, ffn2_gate, ffn2_up, ffn2_down,
           final_norm):
    b, t, d = x.shape
    m = b * t
    x = x.reshape(m, d)
    for l in range(DEPTH):
        x = _ffn(x, ffn1_norm[l], ffn1_gate[l], ffn1_up[l], ffn1_down[l])

        h = _rmsnorm(x, mix_norm[l], BF16)
        proj = _proj(h, w_in[l].astype(BF16))
        y = _mixers(proj.reshape(b, t, PROJ_WIDTH), pool_w[l], pool_scale[l], conv_w[l],
                    sgu_norm[l], sgu_w[l], sgu_b[l])
        x = _residual_matmul(y.reshape(m, d), w_out[l].astype(BF16), x, 1.0)

        x = _ffn(x, ffn2_norm[l], ffn2_gate[l], ffn2_up[l], ffn2_down[l])
    return _rmsnorm(x, final_norm, F32).reshape(b, t, d)
```

```python
import functools

import jax
import jax.numpy as jnp
from jax import lax
from jax.experimental import pallas as pl
from jax.experimental.pallas import tpu as pltpu

D_MODEL = 4096
D_FF = 5632
DEPTH = 4
POOL_GROUPS = 4
POOL_GROUP_DIM = 256
POOL_WIDTH = POOL_GROUPS * POOL_GROUP_DIM
POOL_WINDOWS = (2, 4, 8, 16)
CONV_WIDTH = 1536
CONV_K = 3
SGU_HEAD_DIM = 128
SGU_WIDTH = 1536
SGU_CHUNK = 128
PROJ_WIDTH = POOL_WIDTH + 3 * CONV_WIDTH + 2 * SGU_WIDTH
NORM_EPS = 1e-6
LN_EPS = 1e-5

F32 = jnp.float32
BF16 = jnp.bfloat16

VMEM_LIMIT_BYTES = 56 * 1024 * 1024
NORM_ROWS = 256
MM_ROWS = 1024
MM_COLS = 512
UP_COLS = 256
MIX_COLS = 256
MIX_TIME = 256
POOL_HALO = 16
CONV_HALO = 8
SGU_UNROLL = 2

N_POOL_BLOCKS = POOL_WIDTH // MIX_COLS
N_CONV_BLOCKS = CONV_WIDTH // MIX_COLS
N_SGU_BLOCKS = SGU_WIDTH // MIX_COLS
N_MIX_BLOCKS = N_POOL_BLOCKS + N_CONV_BLOCKS + N_SGU_BLOCKS
SGU_HEADS_PER_BLOCK = MIX_COLS // SGU_HEAD_DIM


def _params(*semantics):
    return pltpu.CompilerParams(dimension_semantics=semantics,
                                vmem_limit_bytes=VMEM_LIMIT_BYTES)


def _row_scale(x):
    return lax.rsqrt(jnp.mean(x * x, axis=-1, keepdims=True) + NORM_EPS)


def _norm_split_kernel(x_ref, g_ref, xg_ref, r_ref):
    x = x_ref[...]
    xg_ref[...] = (x * g_ref[...]).astype(xg_ref.dtype)
    r_ref[...] = _row_scale(x)


def _norm_split(x, g):
    m, d = x.shape
    return pl.pallas_call(
        _norm_split_kernel,
        grid=(m // NORM_ROWS,),
        in_specs=[pl.BlockSpec((NORM_ROWS, d), lambda i: (i, 0)),
                  pl.BlockSpec((1, d), lambda i: (0, 0))],
        out_specs=[pl.BlockSpec((NORM_ROWS, d), lambda i: (i, 0)),
                   pl.BlockSpec((NORM_ROWS, 1), lambda i: (i, 0))],
        out_shape=[jax.ShapeDtypeStruct((m, d), BF16),
                   jax.ShapeDtypeStruct((m, 1), F32)],
        compiler_params=_params("parallel"),
        name="norm_split",
    )(x, g.reshape(1, d))


def _rmsnorm_kernel(x_ref, g_ref, o_ref):
    x = x_ref[...]
    o_ref[...] = x * _row_scale(x) * g_ref[...]


def _rmsnorm(x, g):
    m, d = x.shape
    return pl.pallas_call(
        _rmsnorm_kernel,
        grid=(m // NORM_ROWS,),
        in_specs=[pl.BlockSpec((NORM_ROWS, d), lambda i: (i, 0)),
                  pl.BlockSpec((1, d), lambda i: (0, 0))],
        out_specs=pl.BlockSpec((NORM_ROWS, d), lambda i: (i, 0)),
        out_shape=jax.ShapeDtypeStruct((m, d), F32),
        compiler_params=_params("parallel"),
        name="rmsnorm",
    )(x, g.reshape(1, d))


def _ffn_up_kernel(xg_ref, r_ref, wg_ref, wu_ref, o_ref, wg_s, wu_s):
    @pl.when(pl.program_id(1) == 0)
    def _():
        wg_s[...] = wg_ref[...].astype(BF16)
        wu_s[...] = wu_ref[...].astype(BF16)

    xg = xg_ref[...]
    r = r_ref[...]
    g = jnp.dot(xg, wg_s[...], preferred_element_type=F32) * r
    u = jnp.dot(xg, wu_s[...], preferred_element_type=F32) * r
    o_ref[...] = (g * jax.nn.sigmoid(g) * u).astype(o_ref.dtype)


def _ffn_up(xg, r, wg, wu, layer):
    m, d = xg.shape
    f = wg.shape[2]
    w_spec = pl.BlockSpec((None, d, UP_COLS), lambda j, i: (layer, 0, j))
    return pl.pallas_call(
        _ffn_up_kernel,
        grid=(f // UP_COLS, m // MM_ROWS),
        in_specs=[pl.BlockSpec((MM_ROWS, d), lambda j, i: (i, 0)),
                  pl.BlockSpec((MM_ROWS, 1), lambda j, i: (i, 0)),
                  w_spec, w_spec],
        out_specs=pl.BlockSpec((MM_ROWS, UP_COLS), lambda j, i: (i, j)),
        out_shape=jax.ShapeDtypeStruct((m, f), BF16),
        scratch_shapes=[pltpu.VMEM((d, UP_COLS), BF16), pltpu.VMEM((d, UP_COLS), BF16)],
        compiler_params=_params("parallel", "arbitrary"),
        name="ffn_up",
    )(xg, r, wg, wu)


def _proj_kernel(xg_ref, r_ref, w_ref, o_ref, w_s):
    @pl.when(pl.program_id(1) == 0)
    def _():
        w_s[...] = w_ref[...].astype(BF16)

    o_ref[...] = jnp.dot(xg_ref[...], w_s[...], preferred_element_type=F32) * r_ref[...]


def _proj(xg, r, w, layer):
    m, d = xg.shape
    n = w.shape[2]
    return pl.pallas_call(
        _proj_kernel,
        grid=(n // MM_COLS, m // MM_ROWS),
        in_specs=[pl.BlockSpec((MM_ROWS, d), lambda j, i: (i, 0)),
                  pl.BlockSpec((MM_ROWS, 1), lambda j, i: (i, 0)),
                  pl.BlockSpec((None, d, MM_COLS), lambda j, i: (layer, 0, j))],
        out_specs=pl.BlockSpec((MM_ROWS, MM_COLS), lambda j, i: (i, j)),
        out_shape=jax.ShapeDtypeStruct((m, n), F32),
        scratch_shapes=[pltpu.VMEM((d, MM_COLS), BF16)],
        compiler_params=_params("parallel", "arbitrary"),
        name="proj",
    )(xg, r, w)


def _residual_matmul_kernel(a_ref, w_ref, x_ref, *rest, scale, emit_norm):
    acc = jnp.dot(a_ref[...], w_ref[...], preferred_element_type=F32)
    xn = x_ref[...] + scale * acc
    if not emit_norm:
        (xo_ref,) = rest
        xo_ref[...] = xn
        return
    gn_ref, xo_ref, xg_ref, r_ref, ssq_s = rest
    j = pl.program_id(1)
    xo_ref[...] = xn
    xg_ref[...] = (xn * gn_ref[...]).astype(xg_ref.dtype)
    part = jnp.sum(xn * xn, axis=-1, keepdims=True)

    @pl.when(j == 0)
    def _():
        ssq_s[...] = part

    @pl.when(j > 0)
    def _():
        ssq_s[...] += part

    @pl.when(j == pl.num_programs(1) - 1)
    def _():
        r_ref[...] = lax.rsqrt(ssq_s[...] * (1.0 / D_MODEL) + NORM_EPS)


def _residual_matmul(a, w, layer, x, scale, next_gain):
    m, k = a.shape
    n = w.shape[2]
    emit_norm = next_gain is not None
    tile_spec = pl.BlockSpec((MM_ROWS, MM_COLS), lambda i, j: (i, j))
    operands = [a, w, x]
    in_specs = [pl.BlockSpec((MM_ROWS, k), lambda i, j: (i, 0)),
                pl.BlockSpec((None, k, MM_COLS), lambda i, j: (layer, 0, j)),
                tile_spec]
    out_specs = [tile_spec]
    out_shape = [jax.ShapeDtypeStruct((m, n), F32)]
    scratch = []
    if emit_norm:
        assert n == D_MODEL
        operands.append(next_gain.reshape(1, n))
        in_specs.append(pl.BlockSpec((1, MM_COLS), lambda i, j: (0, j)))
        out_specs += [tile_spec, pl.BlockSpec((MM_ROWS, 1), lambda i, j: (i, 0))]
        out_shape += [jax.ShapeDtypeStruct((m, n), BF16),
                      jax.ShapeDtypeStruct((m, 1), F32)]
        scratch = [pltpu.VMEM((MM_ROWS, 1), F32)]
    outs = pl.pallas_call(
        functools.partial(_residual_matmul_kernel, scale=scale, emit_norm=emit_norm),
        grid=(m // MM_ROWS, n // MM_COLS),
        in_specs=in_specs,
        out_specs=out_specs,
        out_shape=out_shape,
        scratch_shapes=scratch,
        compiler_params=_params("parallel", "arbitrary"),
        name="residual_matmul",
    )(*operands)
    return outs if emit_norm else (outs[0], None, None)


def _gelu_tanh(x):
    return 0.5 * x * (1.0 + jnp.tanh(0.7978845608028654 * (x + 0.044715 * (x * x * x))))


def _time_chunks(ref, halo, n_time, chunk_fn):
    cols = ref.shape[-1]
    first = jnp.concatenate([jnp.zeros((halo, cols), F32), ref[0, 0:MIX_TIME, :]], axis=0)
    chunk_fn(0, first)

    def body(c, carry):
        t0 = pl.multiple_of(c * MIX_TIME, MIX_TIME)
        start = pl.multiple_of(t0 - halo, halo)
        chunk_fn(t0, ref[0, pl.ds(start, MIX_TIME + halo), :])
        return carry

    lax.fori_loop(1, n_time // MIX_TIME, body, 0)


def _pool_block(a_ref, pw_ref, ps_ref, o_ref, window):
    n_time = a_ref.shape[1]
    w = pw_ref[0]
    scale = ps_ref[...]

    def chunk(t0, rows):
        s = rows
        shift = 1
        while shift < window:
            s = s + pltpu.roll(s, shift, axis=0)
            shift *= 2
        t = t0 + lax.broadcasted_iota(jnp.int32, (MIX_TIME, 1), 0)
        count = jnp.minimum(t + 1, window).astype(F32)
        pa = s[POOL_HALO:] / count - rows[POOL_HALO:]
        y = jnp.dot(pa.astype(BF16), w, preferred_element_type=F32) * scale
        o_ref[0, pl.ds(t0, MIX_TIME), :] = y.astype(o_ref.dtype)

    _time_chunks(a_ref, POOL_HALO, n_time, chunk)


def _conv_block(gb_ref, gc_ref, xc_ref, cw_ref, o_ref):
    n_time = gb_ref.shape[1]
    w0 = cw_ref[0:1, :]
    w1 = cw_ref[1:2, :]
    w2 = cw_ref[2:3, :]
    cols = gb_ref.shape[-1]

    def chunk_rows(ref, t0):
        start = pl.multiple_of(t0 - CONV_HALO, CONV_HALO)
        return ref[0, pl.ds(start, MIX_TIME + CONV_HALO), :]

    def compute(t0, z):
        conv = w2 * z + w1 * pltpu.roll(z, 1, axis=0) + w0 * pltpu.roll(z, 2, axis=0)
        y = gb_ref[0, pl.ds(t0, MIX_TIME), :] * conv[CONV_HALO:]
        o_ref[0, pl.ds(t0, MIX_TIME), :] = y.astype(o_ref.dtype)

    z_first = gc_ref[0, 0:MIX_TIME, :] * xc_ref[0, 0:MIX_TIME, :]
    compute(0, jnp.concatenate([jnp.zeros((CONV_HALO, cols), F32), z_first], axis=0))

    def body(c, carry):
        t0 = pl.multiple_of(c * MIX_TIME, MIX_TIME)
        compute(t0, chunk_rows(gc_ref, t0) * chunk_rows(xc_ref, t0))
        return carry

    lax.fori_loop(1, n_time // MIX_TIME, body, 0)


def _sgu_block(u_ref, v_ref, lg_ref, ws_ref, bs_ref, o_ref):
    n_time = u_ref.shape[1]
    row = lax.broadcasted_iota(jnp.int32, (SGU_CHUNK, SGU_CHUNK), 0)
    col = lax.broadcasted_iota(jnp.int32, (SGU_CHUNK, SGU_CHUNK), 1)
    causal = row >= col
    heads = []
    for hh in range(SGU_HEADS_PER_BLOCK):
        lanes = slice(hh * SGU_HEAD_DIM, (hh + 1) * SGU_HEAD_DIM)
        ws = jnp.where(causal, ws_ref[hh], 0.0).astype(BF16)
        heads.append((lanes, ws, bs_ref[hh], lg_ref[:, lanes]))

    def body(n, carry):
        t0 = pl.multiple_of(n * SGU_CHUNK, SGU_CHUNK)
        for lanes, ws, bias, gain in heads:
            v = _gelu_tanh(v_ref[0, pl.ds(t0, SGU_CHUNK), lanes])
            mu = jnp.mean(v, axis=-1, keepdims=True)
            vc = v - mu
            var = jnp.mean(vc * vc, axis=-1, keepdims=True)
            vn = vc * lax.rsqrt(var + LN_EPS) * gain
            mixed = jnp.dot(ws, vn.astype(BF16), preferred_element_type=F32) + bias
            u = _gelu_tanh(u_ref[0, pl.ds(t0, SGU_CHUNK), lanes])
            o_ref[0, pl.ds(t0, SGU_CHUNK), lanes] = (u * mixed).astype(o_ref.dtype)
        return carry

    lax.fori_loop(0, n_time // SGU_CHUNK, body, 0, unroll=SGU_UNROLL)


def _mixer_kernel(a_ref, gb_ref, gc_ref, xc_ref, u_ref, v_ref,
                  pw_ref, ps_ref, cw_ref, lg_ref, ws_ref, bs_ref, o_ref):
    j = pl.program_id(1)
    for g, window in enumerate(POOL_WINDOWS):
        @pl.when(j == g)
        def _(window=window):
            _pool_block(a_ref, pw_ref, ps_ref, o_ref, window)

    @pl.when(jnp.logical_and(j >= N_POOL_BLOCKS, j < N_POOL_BLOCKS + N_CONV_BLOCKS))
    def _():
        _conv_block(gb_ref, gc_ref, xc_ref, cw_ref, o_ref)

    @pl.when(j >= N_POOL_BLOCKS + N_CONV_BLOCKS)
    def _():
        _sgu_block(u_ref, v_ref, lg_ref, ws_ref, bs_ref, o_ref)


def _clamp(j, lo, n):
    return jnp.clip(j - lo, 0, n - 1)


def _mixers(proj, pool_w, pool_scale, conv_w, sgu_norm, sgu_w, sgu_b, layer):
    b, t, _ = proj.shape
    conv_lo = N_POOL_BLOCKS
    sgu_lo = N_POOL_BLOCKS + N_CONV_BLOCKS
    off_gb = POOL_WIDTH // MIX_COLS
    off_gc = off_gb + N_CONV_BLOCKS
    off_xc = off_gc + N_CONV_BLOCKS
    off_u = off_xc + N_CONV_BLOCKS
    off_v = off_u + N_SGU_BLOCKS

    def proj_spec(offset, lo, n):
        return pl.BlockSpec((1, t, MIX_COLS), lambda bi, j: (bi, 0, offset + _clamp(j, lo, n)))

    in_specs = [
        proj_spec(0, 0, N_POOL_BLOCKS),
        proj_spec(off_gb, conv_lo, N_CONV_BLOCKS),
        proj_spec(off_gc, conv_lo, N_CONV_BLOCKS),
        proj_spec(off_xc, conv_lo, N_CONV_BLOCKS),
        proj_spec(off_u, sgu_lo, N_SGU_BLOCKS),
        proj_spec(off_v, sgu_lo, N_SGU_BLOCKS),
        pl.BlockSpec((None, 1, POOL_GROUP_DIM, POOL_GROUP_DIM),
                     lambda bi, j: (layer, _clamp(j, 0, N_POOL_BLOCKS), 0, 0)),
        pl.BlockSpec((None, 1, MIX_COLS),
                     lambda bi, j: (layer, 0, _clamp(j, 0, N_POOL_BLOCKS))),
        pl.BlockSpec((None, CONV_K, MIX_COLS),
                     lambda bi, j: (layer, 0, _clamp(j, conv_lo, N_CONV_BLOCKS))),
        pl.BlockSpec((None, 1, MIX_COLS),
                     lambda bi, j: (layer, 0, _clamp(j, sgu_lo, N_SGU_BLOCKS))),
        pl.BlockSpec((None, SGU_HEADS_PER_BLOCK, SGU_CHUNK, SGU_CHUNK),
                     lambda bi, j: (layer, _clamp(j, sgu_lo, N_SGU_BLOCKS), 0, 0)),
        pl.BlockSpec((None, SGU_HEADS_PER_BLOCK, SGU_CHUNK, 1),
                     lambda bi, j: (layer, _clamp(j, sgu_lo, N_SGU_BLOCKS), 0, 0)),
    ]
    return pl.pallas_call(
        _mixer_kernel,
        grid=(b, N_MIX_BLOCKS),
        in_specs=in_specs,
        out_specs=pl.BlockSpec((1, t, MIX_COLS), lambda bi, j: (bi, 0, j)),
        out_shape=jax.ShapeDtypeStruct((b, t, D_MODEL), BF16),
        compiler_params=_params("parallel", "arbitrary"),
        name="mixers",
    )(proj, proj, proj, proj, proj, proj,
      pool_w, pool_scale, conv_w, sgu_norm, sgu_w, sgu_b)


def kernel(x, ffn1_norm, ffn1_gate, ffn1_up, ffn1_down, mix_norm, w_in, pool_w, pool_scale,
           conv_w, sgu_norm, sgu_w, sgu_b, w_out, ffn2_norm, ffn2_gate, ffn2_up, ffn2_down,
           final_norm):
    b, t, d = x.shape
    m = b * t
    x = x.reshape(m, d)
    ffn1_down = ffn1_down.astype(BF16)
    ffn2_down = ffn2_down.astype(BF16)
    w_out = w_out.astype(BF16)
    pool_w = pool_w.astype(BF16)
    pool_scale = pool_scale[:, None, :]
    sgu_norm = sgu_norm[:, None, :]
    sgu_b = sgu_b[..., None]

    xg, r = _norm_split(x, ffn1_norm[0])
    for l in range(DEPTH):
        act = _ffn_up(xg, r, ffn1_gate, ffn1_up, l)
        x, xg, r = _residual_matmul(act, ffn1_down, l, x, 0.5, mix_norm[l])

        proj = _proj(xg, r, w_in, l)
        y = _mixers(proj.reshape(b, t, PROJ_WIDTH), pool_w, pool_scale, conv_w,
                    sgu_norm, sgu_w, sgu_b, l)
        x, xg, r = _residual_matmul(y.reshape(m, d), w_out, l, x, 1.0, ffn2_norm[l])

        act = _ffn_up(xg, r, ffn2_gate, ffn2_up, l)
        next_gain = ffn1_norm[l + 1] if l + 1 < DEPTH else None
        x, xg, r = _residual_matmul(act, ffn2_down, l, x, 0.5, next_gain)
    return _rmsnorm(x, final_norm).reshape(b, t, d)
```

```python
import functools

import jax
import jax.numpy as jnp
from jax import lax
from jax.experimental import pallas as pl
from jax.experimental.pallas import tpu as pltpu

D_MODEL = 4096
D_FF = 5632
DEPTH = 4
POOL_GROUPS = 4
POOL_GROUP_DIM = 256
POOL_WIDTH = POOL_GROUPS * POOL_GROUP_DIM
POOL_WINDOWS = (2, 4, 8, 16)
CONV_WIDTH = 1536
CONV_K = 3
SGU_HEAD_DIM = 128
SGU_WIDTH = 1536
SGU_CHUNK = 128
PROJ_WIDTH = POOL_WIDTH + 3 * CONV_WIDTH + 2 * SGU_WIDTH
NORM_EPS = 1e-6
LN_EPS = 1e-5

F32 = jnp.float32
BF16 = jnp.bfloat16

VMEM_LIMIT_BYTES = 56 * 1024 * 1024
NORM_ROWS = 256
MM_ROWS = 1024
MM_COLS = 512
MIX_COLS = 256
MIX_TIME = 256
POOL_HALO = 16
CONV_HALO = 8
SGU_UNROLL = 2

N_POOL_BLOCKS = POOL_WIDTH // MIX_COLS
N_CONV_BLOCKS = CONV_WIDTH // MIX_COLS
N_SGU_BLOCKS = SGU_WIDTH // MIX_COLS
N_MIX_BLOCKS = N_POOL_BLOCKS + N_CONV_BLOCKS + N_SGU_BLOCKS
SGU_HEADS_PER_BLOCK = MIX_COLS // SGU_HEAD_DIM


def _params(*semantics):
    return pltpu.CompilerParams(dimension_semantics=semantics,
                                vmem_limit_bytes=VMEM_LIMIT_BYTES)


def _row_scale(x):
    return lax.rsqrt(jnp.mean(x * x, axis=-1, keepdims=True) + NORM_EPS)


def _norm_split_kernel(x_ref, g_ref, xg_ref, r_ref):
    x = x_ref[...]
    xg_ref[...] = (x * g_ref[...]).astype(xg_ref.dtype)
    r_ref[...] = _row_scale(x)


def _norm_split(x, g):
    m, d = x.shape
    return pl.pallas_call(
        _norm_split_kernel,
        grid=(m // NORM_ROWS,),
        in_specs=[pl.BlockSpec((NORM_ROWS, d), lambda i: (i, 0)),
                  pl.BlockSpec((1, d), lambda i: (0, 0))],
        out_specs=[pl.BlockSpec((NORM_ROWS, d), lambda i: (i, 0)),
                   pl.BlockSpec((NORM_ROWS, 1), lambda i: (i, 0))],
        out_shape=[jax.ShapeDtypeStruct((m, d), BF16),
                   jax.ShapeDtypeStruct((m, 1), F32)],
        compiler_params=_params("parallel"),
        name="norm_split",
    )(x, g.reshape(1, d))


def _rmsnorm_kernel(x_ref, g_ref, o_ref):
    x = x_ref[...]
    o_ref[...] = x * _row_scale(x) * g_ref[...]


def _rmsnorm(x, g):
    m, d = x.shape
    return pl.pallas_call(
        _rmsnorm_kernel,
        grid=(m // NORM_ROWS,),
        in_specs=[pl.BlockSpec((NORM_ROWS, d), lambda i: (i, 0)),
                  pl.BlockSpec((1, d), lambda i: (0, 0))],
        out_specs=pl.BlockSpec((NORM_ROWS, d), lambda i: (i, 0)),
        out_shape=jax.ShapeDtypeStruct((m, d), F32),
        compiler_params=_params("parallel"),
        name="rmsnorm",
    )(x, g.reshape(1, d))


def _streamed_matmul_kernel(xg_ref, r_ref, *rest, n_weights, epilogue):
    w_refs = rest[:n_weights]
    o_ref = rest[n_weights]
    w_slots = rest[n_weights + 1:]
    jj = pl.program_id(0)
    chunk = w_refs[0].shape[0]
    rows = pl.ds(pl.multiple_of(pl.program_id(1) * chunk, chunk), chunk)

    def cast_chunk():
        for w_ref, w_s in zip(w_refs, w_slots):
            w_s[jj % 2, rows, :] = w_ref[...].astype(BF16)

    @pl.when(jj == 0)
    def _():
        cast_chunk()

    @pl.when(jj > 0)
    def _():
        cast_chunk()
        xg = xg_ref[...]
        r = r_ref[...]
        cur = (jj - 1) % 2
        accs = [jnp.dot(xg, w_s[cur], preferred_element_type=F32) * r for w_s in w_slots]
        o_ref[...] = epilogue(*accs).astype(o_ref.dtype)


def _streamed_matmul(xg, r, weights, layer, epilogue, out_dtype, name):
    m, k = xg.shape
    n = weights[0].shape[2]
    n_row_tiles = m // MM_ROWS
    n_col_tiles = n // MM_COLS
    chunk = k // n_row_tiles
    assert chunk * n_row_tiles == k and n_col_tiles * MM_COLS == n
    w_spec = pl.BlockSpec((None, chunk, MM_COLS),
                          lambda jj, i: (layer, i, jnp.minimum(jj, n_col_tiles - 1)))
    out_spec = pl.BlockSpec((MM_ROWS, MM_COLS),
                            lambda jj, i: (jnp.where(jj == 0, 0, i), jnp.maximum(jj - 1, 0)))
    return pl.pallas_call(
        functools.partial(_streamed_matmul_kernel, n_weights=len(weights), epilogue=epilogue),
        grid=(n_col_tiles + 1, n_row_tiles),
        in_specs=[pl.BlockSpec((MM_ROWS, k), lambda jj, i: (i, 0)),
                  pl.BlockSpec((MM_ROWS, 1), lambda jj, i: (i, 0))] + [w_spec] * len(weights),
        out_specs=out_spec,
        out_shape=jax.ShapeDtypeStruct((m, n), out_dtype),
        scratch_shapes=[pltpu.VMEM((2, k, MM_COLS), BF16) for _ in weights],
        compiler_params=_params("arbitrary", "arbitrary"),
        name=name,
    )(xg, r, *weights)


def _swiglu_gate(g, u):
    return g * jax.nn.sigmoid(g) * u


def _ffn_up(xg, r, wg, wu, layer):
    return _streamed_matmul(xg, r, [wg, wu], layer, _swiglu_gate, BF16, "ffn_up")


def _proj(xg, r, w, layer):
    return _streamed_matmul(xg, r, [w], layer, lambda acc: acc, F32, "proj")


def _residual_matmul_kernel(a_ref, w_ref, x_ref, *rest, scale, emit_norm):
    acc = jnp.dot(a_ref[...], w_ref[...], preferred_element_type=F32)
    xn = x_ref[...] + scale * acc
    if not emit_norm:
        (xo_ref,) = rest
        xo_ref[...] = xn
        return
    gn_ref, xo_ref, xg_ref, r_ref, ssq_s = rest
    j = pl.program_id(1)
    xo_ref[...] = xn
    xg_ref[...] = (xn * gn_ref[...]).astype(xg_ref.dtype)
    part = jnp.sum(xn * xn, axis=-1, keepdims=True)

    @pl.when(j == 0)
    def _():
        ssq_s[...] = part

    @pl.when(j > 0)
    def _():
        ssq_s[...] += part

    @pl.when(j == pl.num_programs(1) - 1)
    def _():
        r_ref[...] = lax.rsqrt(ssq_s[...] * (1.0 / D_MODEL) + NORM_EPS)


def _residual_matmul(a, w, layer, x, scale, next_gain):
    m, k = a.shape
    n = w.shape[2]
    emit_norm = next_gain is not None
    tile_spec = pl.BlockSpec((MM_ROWS, MM_COLS), lambda i, j: (i, j))
    operands = [a, w, x]
    in_specs = [pl.BlockSpec((MM_ROWS, k), lambda i, j: (i, 0)),
                pl.BlockSpec((None, k, MM_COLS), lambda i, j: (layer, 0, j)),
                tile_spec]
    out_specs = [tile_spec]
    out_shape = [jax.ShapeDtypeStruct((m, n), F32)]
    scratch = []
    if emit_norm:
        assert n == D_MODEL
        operands.append(next_gain.reshape(1, n))
        in_specs.append(pl.BlockSpec((1, MM_COLS), lambda i, j: (0, j)))
        out_specs += [tile_spec, pl.BlockSpec((MM_ROWS, 1), lambda i, j: (i, 0))]
        out_shape += [jax.ShapeDtypeStruct((m, n), BF16),
                      jax.ShapeDtypeStruct((m, 1), F32)]
        scratch = [pltpu.VMEM((MM_ROWS, 1), F32)]
    outs = pl.pallas_call(
        functools.partial(_residual_matmul_kernel, scale=scale, emit_norm=emit_norm),
        grid=(m // MM_ROWS, n // MM_COLS),
        in_specs=in_specs,
        out_specs=out_specs,
        out_shape=out_shape,
        scratch_shapes=scratch,
        compiler_params=_params("parallel", "arbitrary"),
        name="residual_matmul",
    )(*operands)
    return outs if emit_norm else (outs[0], None, None)


def _gelu_tanh(x):
    return 0.5 * x * (1.0 + jnp.tanh(0.7978845608028654 * (x + 0.044715 * (x * x * x))))


def _time_chunks(ref, halo, n_time, chunk_fn):
    cols = ref.shape[-1]
    first = jnp.concatenate([jnp.zeros((halo, cols), F32), ref[0, 0:MIX_TIME, :]], axis=0)
    chunk_fn(0, first)

    def body(c, carry):
        t0 = pl.multiple_of(c * MIX_TIME, MIX_TIME)
        start = pl.multiple_of(t0 - halo, halo)
        chunk_fn(t0, ref[0, pl.ds(start, MIX_TIME + halo), :])
        return carry

    lax.fori_loop(1, n_time // MIX_TIME, body, 0)


def _pool_block(a_ref, pw_ref, ps_ref, o_ref, window):
    n_time = a_ref.shape[1]
    w = pw_ref[0]
    scale = ps_ref[...]

    def chunk(t0, rows):
        s = rows
        shift = 1
        while shift < window:
            s = s + pltpu.roll(s, shift, axis=0)
            shift *= 2
        t = t0 + lax.broadcasted_iota(jnp.int32, (MIX_TIME, 1), 0)
        count = jnp.minimum(t + 1, window).astype(F32)
        pa = s[POOL_HALO:] / count - rows[POOL_HALO:]
        y = jnp.dot(pa.astype(BF16), w, preferred_element_type=F32) * scale
        o_ref[0, pl.ds(t0, MIX_TIME), :] = y.astype(o_ref.dtype)

    _time_chunks(a_ref, POOL_HALO, n_time, chunk)


def _conv_block(gb_ref, gc_ref, xc_ref, cw_ref, o_ref):
    n_time = gb_ref.shape[1]
    w0 = cw_ref[0:1, :]
    w1 = cw_ref[1:2, :]
    w2 = cw_ref[2:3, :]
    cols = gb_ref.shape[-1]

    def chunk_rows(ref, t0):
        start = pl.multiple_of(t0 - CONV_HALO, CONV_HALO)
        return ref[0, pl.ds(start, MIX_TIME + CONV_HALO), :]

    def compute(t0, z):
        conv = w2 * z + w1 * pltpu.roll(z, 1, axis=0) + w0 * pltpu.roll(z, 2, axis=0)
        y = gb_ref[0, pl.ds(t0, MIX_TIME), :] * conv[CONV_HALO:]
        o_ref[0, pl.ds(t0, MIX_TIME), :] = y.astype(o_ref.dtype)

    z_first = gc_ref[0, 0:MIX_TIME, :] * xc_ref[0, 0:MIX_TIME, :]
    compute(0, jnp.concatenate([jnp.zeros((CONV_HALO, cols), F32), z_first], axis=0))

    def body(c, carry):
        t0 = pl.multiple_of(c * MIX_TIME, MIX_TIME)
        compute(t0, chunk_rows(gc_ref, t0) * chunk_rows(xc_ref, t0))
        return carry

    lax.fori_loop(1, n_time // MIX_TIME, body, 0)


def _sgu_block(u_ref, v_ref, lg_ref, ws_ref, bs_ref, o_ref):
    n_time = u_ref.shape[1]
    row = lax.broadcasted_iota(jnp.int32, (SGU_CHUNK, SGU_CHUNK), 0)
    col = lax.broadcasted_iota(jnp.int32, (SGU_CHUNK, SGU_CHUNK), 1)
    causal = row >= col
    heads = []
    for hh in range(SGU_HEADS_PER_BLOCK):
        lanes = slice(hh * SGU_HEAD_DIM, (hh + 1) * SGU_HEAD_DIM)
        ws = jnp.where(causal, ws_ref[hh], 0.0).astype(BF16)
        heads.append((lanes, ws, bs_ref[hh], lg_ref[:, lanes]))

    def body(n, carry):
        t0 = pl.multiple_of(n * SGU_CHUNK, SGU_CHUNK)
        for lanes, ws, bias, gain in heads:
            v = _gelu_tanh(v_ref[0, pl.ds(t0, SGU_CHUNK), lanes])
            mu = jnp.mean(v, axis=-1, keepdims=True)
            vc = v - mu
            var = jnp.mean(vc * vc, axis=-1, keepdims=True)
            vn = vc * lax.rsqrt(var + LN_EPS) * gain
            mixed = jnp.dot(ws, vn.astype(BF16), preferred_element_type=F32) + bias
            u = _gelu_tanh(u_ref[0, pl.ds(t0, SGU_CHUNK), lanes])
            o_ref[0, pl.ds(t0, SGU_CHUNK), lanes] = (u * mixed).astype(o_ref.dtype)
        return carry

    lax.fori_loop(0, n_time // SGU_CHUNK, body, 0, unroll=SGU_UNROLL)


def _mixer_kernel(a_ref, gb_ref, gc_ref, xc_ref, u_ref, v_ref,
                  pw_ref, ps_ref, cw_ref, lg_ref, ws_ref, bs_ref, o_ref):
    j = pl.program_id(1)
    for g, window in enumerate(POOL_WINDOWS):
        @pl.when(j == g)
        def _(window=window):
            _pool_block(a_ref, pw_ref, ps_ref, o_ref, window)

    @pl.when(jnp.logical_and(j >= N_POOL_BLOCKS, j < N_POOL_BLOCKS + N_CONV_BLOCKS))
    def _():
        _conv_block(gb_ref, gc_ref, xc_ref, cw_ref, o_ref)

    @pl.when(j >= N_POOL_BLOCKS + N_CONV_BLOCKS)
    def _():
        _sgu_block(u_ref, v_ref, lg_ref, ws_ref, bs_ref, o_ref)


def _clamp(j, lo, n):
    return jnp.clip(j - lo, 0, n - 1)


def _mixers(proj, pool_w, pool_scale, conv_w, sgu_norm, sgu_w, sgu_b, layer):
    b, t, _ = proj.shape
    conv_lo = N_POOL_BLOCKS
    sgu_lo = N_POOL_BLOCKS + N_CONV_BLOCKS
    off_gb = POOL_WIDTH // MIX_COLS
    off_gc = off_gb + N_CONV_BLOCKS
    off_xc = off_gc + N_CONV_BLOCKS
    off_u = off_xc + N_CONV_BLOCKS
    off_v = off_u + N_SGU_BLOCKS

    def proj_spec(offset, lo, n):
        return pl.BlockSpec((1, t, MIX_COLS), lambda bi, j: (bi, 0, offset + _clamp(j, lo, n)))

    in_specs = [
        proj_spec(0, 0, N_POOL_BLOCKS),
        proj_spec(off_gb, conv_lo, N_CONV_BLOCKS),
        proj_spec(off_gc, conv_lo, N_CONV_BLOCKS),
        proj_spec(off_xc, conv_lo, N_CONV_BLOCKS),
        proj_spec(off_u, sgu_lo, N_SGU_BLOCKS),
        proj_spec(off_v, sgu_lo, N_SGU_BLOCKS),
        pl.BlockSpec((None, 1, POOL_GROUP_DIM, POOL_GROUP_DIM),
                     lambda bi, j: (layer, _clamp(j, 0, N_POOL_BLOCKS), 0, 0)),
        pl.BlockSpec((None, 1, MIX_COLS),
                     lambda bi, j: (layer, 0, _clamp(j, 0, N_POOL_BLOCKS))),
        pl.BlockSpec((None, CONV_K, MIX_COLS),
                     lambda bi, j: (layer, 0, _clamp(j, conv_lo, N_CONV_BLOCKS))),
        pl.BlockSpec((None, 1, MIX_COLS),
                     lambda bi, j: (layer, 0, _clamp(j, sgu_lo, N_SGU_BLOCKS))),
        pl.BlockSpec((None, SGU_HEADS_PER_BLOCK, SGU_CHUNK, SGU_CHUNK),
                     lambda bi, j: (layer, _clamp(j, sgu_lo, N_SGU_BLOCKS), 0, 0)),
        pl.BlockSpec((None, SGU_HEADS_PER_BLOCK, SGU_CHUNK, 1),
                     lambda bi, j: (layer, _clamp(j, sgu_lo, N_SGU_BLOCKS), 0, 0)),
    ]
    return pl.pallas_call(
        _mixer_kernel,
        grid=(b, N_MIX_BLOCKS),
        in_specs=in_specs,
        out_specs=pl.BlockSpec((1, t, MIX_COLS), lambda bi, j: (bi, 0, j)),
        out_shape=jax.ShapeDtypeStruct((b, t, D_MODEL), BF16),
        compiler_params=_params("parallel", "arbitrary"),
        name="mixers",
    )(proj, proj, proj, proj, proj, proj,
      pool_w, pool_scale, conv_w, sgu_norm, sgu_w, sgu_b)


def kernel(x, ffn1_norm, ffn1_gate, ffn1_up, ffn1_down, mix_norm, w_in, pool_w, pool_scale,
           conv_w, sgu_norm, sgu_w, sgu_b, w_out, ffn2_norm, ffn2_gate, ffn2_up, ffn2_down,
           final_norm):
    b, t, d = x.shape
    m = b * t
    x = x.reshape(m, d)
    ffn1_down = ffn1_down.astype(BF16)
    ffn2_down = ffn2_down.astype(BF16)
    w_out = w_out.astype(BF16)
    pool_w = pool_w.astype(BF16)
    pool_scale = pool_scale[:, None, :]
    sgu_norm = sgu_norm[:, None, :]
    sgu_b = sgu_b[..., None]

    xg, r = _norm_split(x, ffn1_norm[0])
    for l in range(DEPTH):
        act = _ffn_up(xg, r, ffn1_gate, ffn1_up, l)
        x, xg, r = _residual_matmul(act, ffn1_down, l, x, 0.5, mix_norm[l])

        proj = _proj(xg, r, w_in, l)
        y = _mixers(proj.reshape(b, t, PROJ_WIDTH), pool_w, pool_scale, conv_w,
                    sgu_norm, sgu_w, sgu_b, l)
        x, xg, r = _residual_matmul(y.reshape(m, d), w_out, l, x, 1.0, ffn2_norm[l])

        act = _ffn_up(xg, r, ffn2_gate, ffn2_up, l)
        next_gain = ffn1_norm[l + 1] if l + 1 < DEPTH else None
        x, xg, r = _residual_matmul(act, ffn2_down, l, x, 0.5, next_gain)
    return _rmsnorm(x, final_norm).reshape(b, t, d)
```

```python
import functools

import jax
import jax.numpy as jnp
from jax import lax
from jax.experimental import pallas as pl
from jax.experimental.pallas import tpu as pltpu

D_MODEL = 4096
D_FF = 5632
DEPTH = 4
POOL_GROUPS = 4
POOL_GROUP_DIM = 256
POOL_WIDTH = POOL_GROUPS * POOL_GROUP_DIM
POOL_WINDOWS = (2, 4, 8, 16)
CONV_WIDTH = 1536
CONV_K = 3
SGU_HEAD_DIM = 128
SGU_WIDTH = 1536
SGU_CHUNK = 128
PROJ_WIDTH = POOL_WIDTH + 3 * CONV_WIDTH + 2 * SGU_WIDTH
NORM_EPS = 1e-6
LN_EPS = 1e-5

F32 = jnp.float32
BF16 = jnp.bfloat16

VMEM_LIMIT_BYTES = 56 * 1024 * 1024
NORM_ROWS = 256
MM_ROWS = 1024
MM_COLS = 512
MIX_COLS = 256
MIX_TIME = 512
BF16_SUBLANES = 16
POOL_HALO = 16
CONV_HALO = 16
SGU_UNROLL = 4

N_POOL_BLOCKS = POOL_WIDTH // MIX_COLS
N_CONV_BLOCKS = CONV_WIDTH // MIX_COLS
N_SGU_BLOCKS = SGU_WIDTH // MIX_COLS
N_MIX_BLOCKS = N_POOL_BLOCKS + N_CONV_BLOCKS + N_SGU_BLOCKS
SGU_HEADS_PER_BLOCK = MIX_COLS // SGU_HEAD_DIM


def _params(*semantics):
    return pltpu.CompilerParams(dimension_semantics=semantics,
                                vmem_limit_bytes=VMEM_LIMIT_BYTES)


def _row_scale(x):
    return lax.rsqrt(jnp.mean(x * x, axis=-1, keepdims=True) + NORM_EPS)


def _norm_split_kernel(x_ref, g_ref, xg_ref, r_ref):
    x = x_ref[...]
    xg_ref[...] = (x * g_ref[...]).astype(xg_ref.dtype)
    r_ref[...] = _row_scale(x)


def _norm_split(x, g):
    m, d = x.shape
    return pl.pallas_call(
        _norm_split_kernel,
        grid=(m // NORM_ROWS,),
        in_specs=[pl.BlockSpec((NORM_ROWS, d), lambda i: (i, 0)),
                  pl.BlockSpec((1, d), lambda i: (0, 0))],
        out_specs=[pl.BlockSpec((NORM_ROWS, d), lambda i: (i, 0)),
                   pl.BlockSpec((NORM_ROWS, 1), lambda i: (i, 0))],
        out_shape=[jax.ShapeDtypeStruct((m, d), BF16),
                   jax.ShapeDtypeStruct((m, 1), F32)],
        compiler_params=_params("parallel"),
        name="norm_split",
    )(x, g.reshape(1, d))


def _rmsnorm_kernel(x_ref, g_ref, o_ref):
    x = x_ref[...]
    o_ref[...] = x * _row_scale(x) * g_ref[...]


def _rmsnorm(x, g):
    m, d = x.shape
    return pl.pallas_call(
        _rmsnorm_kernel,
        grid=(m // NORM_ROWS,),
        in_specs=[pl.BlockSpec((NORM_ROWS, d), lambda i: (i, 0)),
                  pl.BlockSpec((1, d), lambda i: (0, 0))],
        out_specs=pl.BlockSpec((NORM_ROWS, d), lambda i: (i, 0)),
        out_shape=jax.ShapeDtypeStruct((m, d), F32),
        compiler_params=_params("parallel"),
        name="rmsnorm",
    )(x, g.reshape(1, d))


def _streamed_matmul_kernel(xg_ref, r_ref, *rest, n_weights, epilogues):
    w_refs = rest[:n_weights]
    side_in_ref, o_ref, side_out_ref = rest[n_weights:n_weights + 3]
    w_slots = rest[n_weights + 3:]
    jj = pl.program_id(0)
    tile = jj - 1
    chunk = w_refs[0].shape[0]
    rows = pl.ds(pl.multiple_of(pl.program_id(1) * chunk, chunk), chunk)

    def cast_chunk():
        for w_ref, w_s in zip(w_refs, w_slots):
            w_s[jj % 2, rows, :] = w_ref[...].astype(BF16)

    @pl.when(jj == 0)
    def _():
        cast_chunk()

    def step(epilogue):
        cast_chunk()
        side_out_ref[...] = side_in_ref[...].astype(side_out_ref.dtype)
        xg = xg_ref[...]
        r = r_ref[...]
        cur = tile % 2
        accs = [jnp.dot(xg, w_s[cur], preferred_element_type=F32) * r for w_s in w_slots]
        o_ref[...] = epilogue(*accs).astype(o_ref.dtype)

    for idx, (first, epilogue) in enumerate(epilogues):
        cond = tile >= first
        if idx + 1 < len(epilogues):
            cond = jnp.logical_and(cond, tile < epilogues[idx + 1][0])
        pl.when(cond)(functools.partial(step, epilogue))


def _round_up(value, multiple):
    return -(-value // multiple) * multiple


def _streamed_matmul(xg, r, weights, side, layer, epilogues, out_dtype, name):
    m, k = xg.shape
    n = weights[0].shape[2]
    k2, n2 = side.shape[1:]
    n_row_tiles = m // MM_ROWS
    n_col_tiles = n // MM_COLS
    chunk = k // n_row_tiles
    assert chunk * n_row_tiles == k and n_col_tiles * MM_COLS == n
    n_steps = n_col_tiles * n_row_tiles
    side_rows = _round_up(-(-k2 // n_steps), BF16_SUBLANES)
    n_side_blocks = k2 // side_rows
    assert n_side_blocks * side_rows == k2 and n_side_blocks <= n_steps

    def side_block(jj, i):
        return jnp.minimum(jnp.where(jj == 0, 0, (jj - 1) * n_row_tiles + i), n_side_blocks - 1)

    w_spec = pl.BlockSpec((None, chunk, MM_COLS),
                          lambda jj, i: (layer, i, jnp.minimum(jj, n_col_tiles - 1)))
    out_spec = pl.BlockSpec((MM_ROWS, MM_COLS),
                            lambda jj, i: (jnp.where(jj == 0, 0, i), jnp.maximum(jj - 1, 0)))
    return pl.pallas_call(
        functools.partial(_streamed_matmul_kernel, n_weights=len(weights), epilogues=epilogues),
        grid=(n_col_tiles + 1, n_row_tiles),
        in_specs=([pl.BlockSpec((MM_ROWS, k), lambda jj, i: (i, 0)),
                   pl.BlockSpec((MM_ROWS, 1), lambda jj, i: (i, 0))]
                  + [w_spec] * len(weights)
                  + [pl.BlockSpec((None, side_rows, n2),
                                  lambda jj, i: (layer, side_block(jj, i), 0))]),
        out_specs=[out_spec,
                   pl.BlockSpec((side_rows, n2), lambda jj, i: (side_block(jj, i), 0))],
        out_shape=[jax.ShapeDtypeStruct((m, n), out_dtype),
                   jax.ShapeDtypeStruct((k2, n2), BF16)],
        scratch_shapes=[pltpu.VMEM((2, k, MM_COLS), BF16) for _ in weights],
        compiler_params=_params("arbitrary", "arbitrary"),
        name=name,
    )(xg, r, *weights, side)


def _swiglu_gate(g, u):
    return g * jax.nn.sigmoid(g) * u


def _gelu_tanh(x):
    return 0.5 * x * (1.0 + jnp.tanh(0.7978845608028654 * (x + 0.044715 * (x * x * x))))


def _ffn_up(xg, r, wg, wu, wd, layer):
    return _streamed_matmul(xg, r, [wg, wu], wd, layer, ((0, _swiglu_gate),), BF16, "ffn_up")


def _proj(xg, r, w, w_next, layer):
    first_sgu_tile, rem = divmod(PROJ_WIDTH - 2 * SGU_WIDTH, MM_COLS)
    assert rem == 0
    epilogues = ((0, lambda acc: acc), (first_sgu_tile, _gelu_tanh))
    return _streamed_matmul(xg, r, [w], w_next, layer, epilogues, BF16, "proj")


def _residual_matmul_kernel(a_ref, w_ref, x_ref, *rest, scale, emit_norm):
    acc = jnp.dot(a_ref[...], w_ref[...], preferred_element_type=F32)
    xn = x_ref[...] + scale * acc
    if not emit_norm:
        (xo_ref,) = rest
        xo_ref[...] = xn
        return
    gn_ref, xo_ref, xg_ref, r_ref, ssq_s = rest
    j = pl.program_id(1)
    xo_ref[...] = xn
    xg_ref[...] = (xn * gn_ref[...]).astype(xg_ref.dtype)
    part = jnp.sum(xn * xn, axis=-1, keepdims=True)

    @pl.when(j == 0)
    def _():
        ssq_s[...] = part

    @pl.when(j > 0)
    def _():
        ssq_s[...] += part

    @pl.when(j == pl.num_programs(1) - 1)
    def _():
        r_ref[...] = lax.rsqrt(ssq_s[...] * (1.0 / D_MODEL) + NORM_EPS)


def _residual_matmul(a, w, x, scale, next_gain):
    m, k = a.shape
    n = w.shape[1]
    emit_norm = next_gain is not None
    tile_spec = pl.BlockSpec((MM_ROWS, MM_COLS), lambda i, j: (i, j))
    operands = [a, w, x]
    in_specs = [pl.BlockSpec((MM_ROWS, k), lambda i, j: (i, 0)),
                pl.BlockSpec((k, MM_COLS), lambda i, j: (0, j)),
                tile_spec]
    out_specs = [tile_spec]
    out_shape = [jax.ShapeDtypeStruct((m, n), F32)]
    scratch = []
    if emit_norm:
        assert n == D_MODEL
        operands.append(next_gain.reshape(1, n))
        in_specs.append(pl.BlockSpec((1, MM_COLS), lambda i, j: (0, j)))
        out_specs += [tile_spec, pl.BlockSpec((MM_ROWS, 1), lambda i, j: (i, 0))]
        out_shape += [jax.ShapeDtypeStruct((m, n), BF16),
                      jax.ShapeDtypeStruct((m, 1), F32)]
        scratch = [pltpu.VMEM((MM_ROWS, 1), F32)]
    outs = pl.pallas_call(
        functools.partial(_residual_matmul_kernel, scale=scale, emit_norm=emit_norm),
        grid=(m // MM_ROWS, n // MM_COLS),
        in_specs=in_specs,
        out_specs=out_specs,
        out_shape=out_shape,
        scratch_shapes=scratch,
        compiler_params=_params("parallel", "arbitrary"),
        name="residual_matmul",
    )(*operands)
    return outs if emit_norm else (outs[0], None, None)


def _time_chunks(ref, halo, n_time, chunk_fn):
    cols = ref.shape[-1]
    rows0 = jnp.concatenate([jnp.zeros((halo, cols), F32),
                             ref[0, 0:MIX_TIME, :].astype(F32)], axis=0)
    chunk_fn(0, rows0, True)

    def body(c, carry):
        t0 = pl.multiple_of(c * MIX_TIME, MIX_TIME)
        start = pl.multiple_of(t0 - halo, halo)
        chunk_fn(t0, ref[0, pl.ds(start, MIX_TIME + halo), :].astype(F32), False)
        return carry

    lax.fori_loop(1, n_time // MIX_TIME, body, 0)


def _pool_block(a_ref, pw_ref, ps_ref, o_ref, window):
    n_time = a_ref.shape[1]
    w = pw_ref[0]
    scale = ps_ref[...]

    def chunk(t0, rows, first):
        s = rows
        shift = 1
        while shift < window:
            s = s + pltpu.roll(s, shift, axis=0)
            shift *= 2
        if first:
            t = lax.broadcasted_iota(jnp.int32, (MIX_TIME, 1), 0)
            pooled = s[POOL_HALO:] / jnp.minimum(t + 1, window).astype(F32)
        else:
            pooled = s[POOL_HALO:] * (1.0 / window)
        pa = pooled - rows[POOL_HALO:]
        y = jnp.dot(pa.astype(BF16), w, preferred_element_type=F32) * scale
        o_ref[0, pl.ds(t0, MIX_TIME), :] = y.astype(o_ref.dtype)

    _time_chunks(a_ref, POOL_HALO, n_time, chunk)


def _conv_block(gb_ref, gc_ref, xc_ref, cw_ref, o_ref):
    n_time = gb_ref.shape[1]
    w0 = cw_ref[0:1, :]
    w1 = cw_ref[1:2, :]
    w2 = cw_ref[2:3, :]
    cols = gb_ref.shape[-1]

    def chunk_rows(ref, t0):
        start = pl.multiple_of(t0 - CONV_HALO, CONV_HALO)
        return ref[0, pl.ds(start, MIX_TIME + CONV_HALO), :].astype(F32)

    def compute(t0, z):
        conv = w2 * z + w1 * pltpu.roll(z, 1, axis=0) + w0 * pltpu.roll(z, 2, axis=0)
        y = gb_ref[0, pl.ds(t0, MIX_TIME), :].astype(F32) * conv[CONV_HALO:]
        o_ref[0, pl.ds(t0, MIX_TIME), :] = y.astype(o_ref.dtype)

    z_first = gc_ref[0, 0:MIX_TIME, :].astype(F32) * xc_ref[0, 0:MIX_TIME, :].astype(F32)
    compute(0, jnp.concatenate([jnp.zeros((CONV_HALO, cols), F32), z_first], axis=0))

    def body(c, carry):
        t0 = pl.multiple_of(c * MIX_TIME, MIX_TIME)
        compute(t0, chunk_rows(gc_ref, t0) * chunk_rows(xc_ref, t0))
        return carry

    lax.fori_loop(1, n_time // MIX_TIME, body, 0)


def _sgu_block(u_ref, v_ref, lg_ref, ws_ref, bs_ref, o_ref):
    n_time = u_ref.shape[1]
    row = lax.broadcasted_iota(jnp.int32, (SGU_CHUNK, SGU_CHUNK), 0)
    col = lax.broadcasted_iota(jnp.int32, (SGU_CHUNK, SGU_CHUNK), 1)
    causal = row >= col
    heads = []
    for hh in range(SGU_HEADS_PER_BLOCK):
        lanes = slice(hh * SGU_HEAD_DIM, (hh + 1) * SGU_HEAD_DIM)
        ws = jnp.where(causal, ws_ref[hh], 0.0).astype(BF16)
        heads.append((lanes, ws, bs_ref[hh], lg_ref[:, lanes]))

    def body(n, carry):
        t0 = pl.multiple_of(n * SGU_CHUNK, SGU_CHUNK)
        for lanes, ws, bias, gain in heads:
            v = v_ref[0, pl.ds(t0, SGU_CHUNK), lanes].astype(F32)
            mu = jnp.mean(v, axis=-1, keepdims=True)
            vc = v - mu
            var = jnp.mean(vc * vc, axis=-1, keepdims=True)
            vn = vc * lax.rsqrt(var + LN_EPS) * gain
            mixed = jnp.dot(ws, vn.astype(BF16), preferred_element_type=F32) + bias
            u = u_ref[0, pl.ds(t0, SGU_CHUNK), lanes].astype(F32)
            o_ref[0, pl.ds(t0, SGU_CHUNK), lanes] = (u * mixed).astype(o_ref.dtype)
        return carry

    lax.fori_loop(0, n_time // SGU_CHUNK, body, 0, unroll=SGU_UNROLL)


def _mixer_kernel(a_ref, gb_ref, gc_ref, xc_ref, u_ref, v_ref,
                  pw_ref, ps_ref, cw_ref, lg_ref, ws_ref, bs_ref, o_ref):
    j = pl.program_id(1)
    for g, window in enumerate(POOL_WINDOWS):
        @pl.when(j == g)
        def _(window=window):
            _pool_block(a_ref, pw_ref, ps_ref, o_ref, window)

    @pl.when(jnp.logical_and(j >= N_POOL_BLOCKS, j < N_POOL_BLOCKS + N_CONV_BLOCKS))
    def _():
        _conv_block(gb_ref, gc_ref, xc_ref, cw_ref, o_ref)

    @pl.when(j >= N_POOL_BLOCKS + N_CONV_BLOCKS)
    def _():
        _sgu_block(u_ref, v_ref, lg_ref, ws_ref, bs_ref, o_ref)


def _clamp(j, lo, n):
    return jnp.clip(j - lo, 0, n - 1)


def _mixers(proj, pool_w, pool_scale, conv_w, sgu_norm, sgu_w, sgu_b, layer):
    b, t, _ = proj.shape
    conv_lo = N_POOL_BLOCKS
    sgu_lo = N_POOL_BLOCKS + N_CONV_BLOCKS
    off_gb = POOL_WIDTH // MIX_COLS
    off_gc = off_gb + N_CONV_BLOCKS
    off_xc = off_gc + N_CONV_BLOCKS
    off_u = off_xc + N_CONV_BLOCKS
    off_v = off_u + N_SGU_BLOCKS

    def proj_spec(offset, lo, n):
        return pl.BlockSpec((1, t, MIX_COLS), lambda bi, j: (bi, 0, offset + _clamp(j, lo, n)))

    in_specs = [
        proj_spec(0, 0, N_POOL_BLOCKS),
        proj_spec(off_gb, conv_lo, N_CONV_BLOCKS),
        proj_spec(off_gc, conv_lo, N_CONV_BLOCKS),
        proj_spec(off_xc, conv_lo, N_CONV_BLOCKS),
        proj_spec(off_u, sgu_lo, N_SGU_BLOCKS),
        proj_spec(off_v, sgu_lo, N_SGU_BLOCKS),
        pl.BlockSpec((None, 1, POOL_GROUP_DIM, POOL_GROUP_DIM),
                     lambda bi, j: (layer, _clamp(j, 0, N_POOL_BLOCKS), 0, 0)),
        pl.BlockSpec((None, 1, MIX_COLS),
                     lambda bi, j: (layer, 0, _clamp(j, 0, N_POOL_BLOCKS))),
        pl.BlockSpec((None, CONV_K, MIX_COLS),
                     lambda bi, j: (layer, 0, _clamp(j, conv_lo, N_CONV_BLOCKS))),
        pl.BlockSpec((None, 1, MIX_COLS),
                     lambda bi, j: (layer, 0, _clamp(j, sgu_lo, N_SGU_BLOCKS))),
        pl.BlockSpec((None, SGU_HEADS_PER_BLOCK, SGU_CHUNK, SGU_CHUNK),
                     lambda bi, j: (layer, _clamp(j, sgu_lo, N_SGU_BLOCKS), 0, 0)),
        pl.BlockSpec((None, SGU_HEADS_PER_BLOCK, SGU_CHUNK, 1),
                     lambda bi, j: (layer, _clamp(j, sgu_lo, N_SGU_BLOCKS), 0, 0)),
    ]
    return pl.pallas_call(
        _mixer_kernel,
        grid=(b, N_MIX_BLOCKS),
        in_specs=in_specs,
        out_specs=pl.BlockSpec((1, t, MIX_COLS), lambda bi, j: (bi, 0, j)),
        out_shape=jax.ShapeDtypeStruct((b, t, D_MODEL), BF16),
        compiler_params=_params("parallel", "arbitrary"),
        name="mixers",
    )(proj, proj, proj, proj, proj, proj,
      pool_w, pool_scale, conv_w, sgu_norm, sgu_w, sgu_b)


def kernel(x, ffn1_norm, ffn1_gate, ffn1_up, ffn1_down, mix_norm, w_in, pool_w, pool_scale,
           conv_w, sgu_norm, sgu_w, sgu_b, w_out, ffn2_norm, ffn2_gate, ffn2_up, ffn2_down,
           final_norm):
    b, t, d = x.shape
    m = b * t
    x = x.reshape(m, d)
    pool_w = pool_w.astype(BF16)
    pool_scale = pool_scale[:, None, :]
    sgu_norm = sgu_norm[:, None, :]
    sgu_b = sgu_b[..., None]

    xg, r = _norm_split(x, ffn1_norm[0])
    for l in range(DEPTH):
        act, wd = _ffn_up(xg, r, ffn1_gate, ffn1_up, ffn1_down, l)
        x, xg, r = _residual_matmul(act, wd, x, 0.5, mix_norm[l])

        proj, wo = _proj(xg, r, w_in, w_out, l)
        y = _mixers(proj.reshape(b, t, PROJ_WIDTH), pool_w, pool_scale, conv_w,
                    sgu_norm, sgu_w, sgu_b, l)
        x, xg, r = _residual_matmul(y.reshape(m, d), wo, x, 1.0, ffn2_norm[l])

        act, wd = _ffn_up(xg, r, ffn2_gate, ffn2_up, ffn2_down, l)
        next_gain = ffn1_norm[l + 1] if l + 1 < DEPTH else None
        x, xg, r = _residual_matmul(act, wd, x, 0.5, next_gain)
    return _rmsnorm(x, final_norm).reshape(b, t, d)
```

```python
import functools

import jax
import jax.numpy as jnp
from jax import lax
from jax.experimental import pallas as pl
from jax.experimental.pallas import tpu as pltpu

D_MODEL = 4096
D_FF = 5632
DEPTH = 4
POOL_GROUPS = 4
POOL_GROUP_DIM = 256
POOL_WIDTH = POOL_GROUPS * POOL_GROUP_DIM
POOL_WINDOWS = (2, 4, 8, 16)
CONV_WIDTH = 1536
CONV_K = 3
SGU_HEAD_DIM = 128
SGU_WIDTH = 1536
SGU_CHUNK = 128
PROJ_WIDTH = POOL_WIDTH + 3 * CONV_WIDTH + 2 * SGU_WIDTH
NORM_EPS = 1e-6
LN_EPS = 1e-5

F32 = jnp.float32
BF16 = jnp.bfloat16

VMEM_LIMIT_BYTES = 56 * 1024 * 1024
NORM_ROWS = 256
MM_ROWS = 1024
MM_COLS = 512
ROW_SPLITS = 4
MIX_COLS = 256
MIX_TIME = 512
BF16_SUBLANES = 16
POOL_HALO = 16
CONV_HALO = 16
SGU_UNROLL = 4

N_POOL_BLOCKS = POOL_WIDTH // MIX_COLS
N_CONV_BLOCKS = CONV_WIDTH // MIX_COLS
N_SGU_BLOCKS = SGU_WIDTH // MIX_COLS
N_MIX_BLOCKS = N_POOL_BLOCKS + N_CONV_BLOCKS + N_SGU_BLOCKS
SGU_HEADS_PER_BLOCK = MIX_COLS // SGU_HEAD_DIM


def _params(*semantics):
    return pltpu.CompilerParams(dimension_semantics=semantics,
                                vmem_limit_bytes=VMEM_LIMIT_BYTES)


def _row_scale(x):
    return lax.rsqrt(jnp.mean(x * x, axis=-1, keepdims=True) + NORM_EPS)


def _norm_split_kernel(x_ref, g_ref, xg_ref, r_ref):
    x = x_ref[...]
    xg_ref[...] = (x * g_ref[...]).astype(xg_ref.dtype)
    r_ref[...] = _row_scale(x)


def _norm_split(x, g):
    m, d = x.shape
    return pl.pallas_call(
        _norm_split_kernel,
        grid=(m // NORM_ROWS,),
        in_specs=[pl.BlockSpec((NORM_ROWS, d), lambda i: (i, 0)),
                  pl.BlockSpec((1, d), lambda i: (0, 0))],
        out_specs=[pl.BlockSpec((NORM_ROWS, d), lambda i: (i, 0)),
                   pl.BlockSpec((NORM_ROWS, 1), lambda i: (i, 0))],
        out_shape=[jax.ShapeDtypeStruct((m, d), BF16),
                   jax.ShapeDtypeStruct((m, 1), F32)],
        compiler_params=_params("parallel"),
        name="norm_split",
    )(x, g.reshape(1, d))


def _rmsnorm_kernel(x_ref, g_ref, o_ref):
    x = x_ref[...]
    o_ref[...] = x * _row_scale(x) * g_ref[...]


def _rmsnorm(x, g):
    m, d = x.shape
    return pl.pallas_call(
        _rmsnorm_kernel,
        grid=(m // NORM_ROWS,),
        in_specs=[pl.BlockSpec((NORM_ROWS, d), lambda i: (i, 0)),
                  pl.BlockSpec((1, d), lambda i: (0, 0))],
        out_specs=pl.BlockSpec((NORM_ROWS, d), lambda i: (i, 0)),
        out_shape=jax.ShapeDtypeStruct((m, d), F32),
        compiler_params=_params("parallel"),
        name="rmsnorm",
    )(x, g.reshape(1, d))


def _streamed_matmul_kernel(xg_ref, r_ref, *rest, n_weights, epilogues):
    w_refs = rest[:n_weights]
    side_in_ref, o_ref, side_out_ref = rest[n_weights:n_weights + 3]
    w_slots = rest[n_weights + 3:]
    jj = pl.program_id(0)
    tile = jj - 1
    chunk = w_refs[0].shape[0]
    rows = pl.ds(pl.multiple_of(pl.program_id(1) * chunk, chunk), chunk)

    def cast_chunk():
        for w_ref, w_s in zip(w_refs, w_slots):
            w_s[jj % 2, rows, :] = w_ref[...].astype(BF16)

    @pl.when(jj == 0)
    def _():
        cast_chunk()

    def step(epilogue):
        cast_chunk()
        side_out_ref[...] = side_in_ref[...].astype(side_out_ref.dtype)
        xg = xg_ref[...]
        r = r_ref[...]
        cur = tile % 2
        accs = [jnp.dot(xg, w_s[cur], preferred_element_type=F32) * r for w_s in w_slots]
        o_ref[...] = epilogue(*accs).astype(o_ref.dtype)

    for idx, (first, epilogue) in enumerate(epilogues):
        cond = tile >= first
        if idx + 1 < len(epilogues):
            cond = jnp.logical_and(cond, tile < epilogues[idx + 1][0])
        pl.when(cond)(functools.partial(step, epilogue))


def _round_up(value, multiple):
    return -(-value // multiple) * multiple


def _streamed_matmul(xg, r, weights, side, layer, epilogues, out_dtype, name):
    m, k = xg.shape
    n = weights[0].shape[2]
    k2, n2 = side.shape[1:]
    n_row_tiles = m // MM_ROWS
    n_col_tiles = n // MM_COLS
    chunk = k // n_row_tiles
    assert chunk * n_row_tiles == k and n_col_tiles * MM_COLS == n
    n_steps = n_col_tiles * n_row_tiles
    side_rows = _round_up(-(-k2 // n_steps), BF16_SUBLANES)
    n_side_blocks = k2 // side_rows
    assert n_side_blocks * side_rows == k2 and n_side_blocks <= n_steps

    def side_block(jj, i):
        return jnp.minimum(jnp.where(jj == 0, 0, (jj - 1) * n_row_tiles + i), n_side_blocks - 1)

    w_spec = pl.BlockSpec((None, chunk, MM_COLS),
                          lambda jj, i: (layer, i, jnp.minimum(jj, n_col_tiles - 1)))
    def row_tile(jj, i):
        return jnp.where(jj == 0, 0, i)

    out_spec = pl.BlockSpec((MM_ROWS, MM_COLS),
                            lambda jj, i: (row_tile(jj, i), jnp.maximum(jj - 1, 0)))
    return pl.pallas_call(
        functools.partial(_streamed_matmul_kernel, n_weights=len(weights), epilogues=epilogues),
        grid=(n_col_tiles + 1, n_row_tiles),
        in_specs=([pl.BlockSpec((MM_ROWS, k), lambda jj, i: (row_tile(jj, i), 0)),
                   pl.BlockSpec((MM_ROWS, 1), lambda jj, i: (row_tile(jj, i), 0))]
                  + [w_spec] * len(weights)
                  + [pl.BlockSpec((None, side_rows, n2),
                                  lambda jj, i: (layer, side_block(jj, i), 0))]),
        out_specs=[out_spec,
                   pl.BlockSpec((side_rows, n2), lambda jj, i: (side_block(jj, i), 0))],
        out_shape=[jax.ShapeDtypeStruct((m, n), out_dtype),
                   jax.ShapeDtypeStruct((k2, n2), BF16)],
        scratch_shapes=[pltpu.VMEM((2, k, MM_COLS), BF16) for _ in weights],
        compiler_params=_params("arbitrary", "arbitrary"),
        name=name,
    )(xg, r, *weights, side)


def _swiglu_gate(g, u):
    return g * jax.nn.sigmoid(g) * u


def _gelu_tanh(x):
    return 0.5 * x * (1.0 + jnp.tanh(0.7978845608028654 * (x + 0.044715 * (x * x * x))))


def _ffn_up(xg, r, wg, wu, wd, layer):
    return _streamed_matmul(xg, r, [wg, wu], wd, layer, ((0, _swiglu_gate),), BF16, "ffn_up")


def _proj(xg, r, w, w_next, layer):
    first_sgu_tile, rem = divmod(PROJ_WIDTH - 2 * SGU_WIDTH, MM_COLS)
    assert rem == 0
    epilogues = ((0, lambda acc: acc), (first_sgu_tile, _gelu_tanh))
    return _streamed_matmul(xg, r, [w], w_next, layer, epilogues, BF16, "proj")


def _residual_matmul_kernel(a_ref, w_ref, x_ref, *rest, scale, emit_norm):
    if emit_norm:
        gn_ref, xo_ref, xg_ref, r_ref, ssq_s = rest
        j = pl.program_id(1)

        @pl.when(j == 0)
        def _():
            ssq_s[...] = jnp.zeros_like(ssq_s)
    else:
        (xo_ref,) = rest

    sub = a_ref.shape[0] // ROW_SPLITS
    for h in range(ROW_SPLITS):
        rows = slice(h * sub, (h + 1) * sub)
        acc = jnp.dot(a_ref[rows, :], w_ref[...], preferred_element_type=F32)
        xn = x_ref[rows, :] + scale * acc
        xo_ref[rows, :] = xn
        if emit_norm:
            xg_ref[rows, :] = (xn * gn_ref[...]).astype(xg_ref.dtype)
            ssq_s[rows, :] += jnp.sum(xn * xn, axis=-1, keepdims=True)

    if emit_norm:
        @pl.when(j == pl.num_programs(1) - 1)
        def _():
            r_ref[...] = lax.rsqrt(ssq_s[...] * (1.0 / D_MODEL) + NORM_EPS)


def _residual_matmul(a, w, x, scale, next_gain):
    m, k = a.shape
    n = w.shape[1]
    emit_norm = next_gain is not None
    tile_spec = pl.BlockSpec((MM_ROWS, MM_COLS), lambda i, j: (i, j))
    operands = [a, w, x]
    in_specs = [pl.BlockSpec((MM_ROWS, k), lambda i, j: (i, 0)),
                pl.BlockSpec((k, MM_COLS), lambda i, j: (0, j)),
                tile_spec]
    out_specs = [tile_spec]
    out_shape = [jax.ShapeDtypeStruct((m, n), F32)]
    scratch = []
    if emit_norm:
        assert n == D_MODEL
        operands.append(next_gain.reshape(1, n))
        in_specs.append(pl.BlockSpec((1, MM_COLS), lambda i, j: (0, j)))
        out_specs += [tile_spec, pl.BlockSpec((MM_ROWS, 1), lambda i, j: (i, 0))]
        out_shape += [jax.ShapeDtypeStruct((m, n), BF16),
                      jax.ShapeDtypeStruct((m, 1), F32)]
        scratch = [pltpu.VMEM((MM_ROWS, 1), F32)]
    outs = pl.pallas_call(
        functools.partial(_residual_matmul_kernel, scale=scale, emit_norm=emit_norm),
        grid=(m // MM_ROWS, n // MM_COLS),
        in_specs=in_specs,
        out_specs=out_specs,
        out_shape=out_shape,
        scratch_shapes=scratch,
        compiler_params=_params("parallel", "arbitrary"),
        name="residual_matmul",
    )(*operands)
    return outs if emit_norm else (outs[0], None, None)


def _time_chunks(ref, halo, n_time, chunk_fn):
    cols = ref.shape[-1]
    rows0 = jnp.concatenate([jnp.zeros((halo, cols), F32),
                             ref[0, 0:MIX_TIME, :].astype(F32)], axis=0)
    chunk_fn(0, rows0, True)

    def body(c, carry):
        t0 = pl.multiple_of(c * MIX_TIME, MIX_TIME)
        start = pl.multiple_of(t0 - halo, halo)
        chunk_fn(t0, ref[0, pl.ds(start, MIX_TIME + halo), :].astype(F32), False)
        return carry

    lax.fori_loop(1, n_time // MIX_TIME, body, 0)


def _pool_block(a_ref, pw_ref, ps_ref, o_ref, window):
    n_time = a_ref.shape[1]
    w = pw_ref[0]
    scale = ps_ref[...]

    def chunk(t0, rows, first):
        s = rows
        shift = 1
        while shift < window:
            s = s + pltpu.roll(s, shift, axis=0)
            shift *= 2
        if first:
            t = lax.broadcasted_iota(jnp.int32, (MIX_TIME, 1), 0)
            pooled = s[POOL_HALO:] / jnp.minimum(t + 1, window).astype(F32)
        else:
            pooled = s[POOL_HALO:] * (1.0 / window)
        pa = pooled - rows[POOL_HALO:]
        y = jnp.dot(pa.astype(BF16), w, preferred_element_type=F32) * scale
        o_ref[0, pl.ds(t0, MIX_TIME), :] = y.astype(o_ref.dtype)

    _time_chunks(a_ref, POOL_HALO, n_time, chunk)


def _conv_block(gb_ref, gc_ref, xc_ref, cw_ref, o_ref):
    n_time = gb_ref.shape[1]
    w0 = cw_ref[0:1, :]
    w1 = cw_ref[1:2, :]
    w2 = cw_ref[2:3, :]
    cols = gb_ref.shape[-1]

    def chunk_rows(ref, t0):
        start = pl.multiple_of(t0 - CONV_HALO, CONV_HALO)
        return ref[0, pl.ds(start, MIX_TIME + CONV_HALO), :].astype(F32)

    def compute(t0, z):
        conv = w2 * z + w1 * pltpu.roll(z, 1, axis=0) + w0 * pltpu.roll(z, 2, axis=0)
        y = gb_ref[0, pl.ds(t0, MIX_TIME), :].astype(F32) * conv[CONV_HALO:]
        o_ref[0, pl.ds(t0, MIX_TIME), :] = y.astype(o_ref.dtype)

    z_first = gc_ref[0, 0:MIX_TIME, :].astype(F32) * xc_ref[0, 0:MIX_TIME, :].astype(F32)
    compute(0, jnp.concatenate([jnp.zeros((CONV_HALO, cols), F32), z_first], axis=0))

    def body(c, carry):
        t0 = pl.multiple_of(c * MIX_TIME, MIX_TIME)
        compute(t0, chunk_rows(gc_ref, t0) * chunk_rows(xc_ref, t0))
        return carry

    lax.fori_loop(1, n_time // MIX_TIME, body, 0)


def _sgu_block(u_ref, v_ref, lg_ref, ws_ref, bs_ref, o_ref):
    n_time = u_ref.shape[1]
    row = lax.broadcasted_iota(jnp.int32, (SGU_CHUNK, SGU_CHUNK), 0)
    col = lax.broadcasted_iota(jnp.int32, (SGU_CHUNK, SGU_CHUNK), 1)
    causal = row >= col
    heads = []
    for hh in range(SGU_HEADS_PER_BLOCK):
        lanes = slice(hh * SGU_HEAD_DIM, (hh + 1) * SGU_HEAD_DIM)
        ws = jnp.where(causal, ws_ref[hh], 0.0).astype(BF16)
        heads.append((lanes, ws, bs_ref[hh], lg_ref[:, lanes]))

    def body(n, carry):
        t0 = pl.multiple_of(n * SGU_CHUNK, SGU_CHUNK)
        for lanes, ws, bias, gain in heads:
            v = v_ref[0, pl.ds(t0, SGU_CHUNK), lanes].astype(F32)
            mu = jnp.mean(v, axis=-1, keepdims=True)
            vc = v - mu
            var = jnp.mean(vc * vc, axis=-1, keepdims=True)
            vn = vc * lax.rsqrt(var + LN_EPS) * gain
            mixed = jnp.dot(ws, vn.astype(BF16), preferred_element_type=F32) + bias
            u = u_ref[0, pl.ds(t0, SGU_CHUNK), lanes].astype(F32)
            o_ref[0, pl.ds(t0, SGU_CHUNK), lanes] = (u * mixed).astype(o_ref.dtype)
        return carry

    lax.fori_loop(0, n_time // SGU_CHUNK, body, 0, unroll=SGU_UNROLL)


def _mixer_kernel(a_ref, gb_ref, gc_ref, xc_ref, u_ref, v_ref,
                  pw_ref, ps_ref, cw_ref, lg_ref, ws_ref, bs_ref, o_ref):
    j = pl.program_id(1)
    for g, window in enumerate(POOL_WINDOWS):
        @pl.when(j == g)
        def _(window=window):
            _pool_block(a_ref, pw_ref, ps_ref, o_ref, window)

    @pl.when(jnp.logical_and(j >= N_POOL_BLOCKS, j < N_POOL_BLOCKS + N_CONV_BLOCKS))
    def _():
        _conv_block(gb_ref, gc_ref, xc_ref, cw_ref, o_ref)

    @pl.when(j >= N_POOL_BLOCKS + N_CONV_BLOCKS)
    def _():
        _sgu_block(u_ref, v_ref, lg_ref, ws_ref, bs_ref, o_ref)


def _clamp(j, lo, n):
    return jnp.clip(j - lo, 0, n - 1)


def _mixers(proj, pool_w, pool_scale, conv_w, sgu_norm, sgu_w, sgu_b, layer):
    b, t, _ = proj.shape
    conv_lo = N_POOL_BLOCKS
    sgu_lo = N_POOL_BLOCKS + N_CONV_BLOCKS
    off_gb = POOL_WIDTH // MIX_COLS
    off_gc = off_gb + N_CONV_BLOCKS
    off_xc = off_gc + N_CONV_BLOCKS
    off_u = off_xc + N_CONV_BLOCKS
    off_v = off_u + N_SGU_BLOCKS

    def proj_spec(offset, lo, n):
        return pl.BlockSpec((1, t, MIX_COLS), lambda bi, j: (bi, 0, offset + _clamp(j, lo, n)))

    in_specs = [
        proj_spec(0, 0, N_POOL_BLOCKS),
        proj_spec(off_gb, conv_lo, N_CONV_BLOCKS),
        proj_spec(off_gc, conv_lo, N_CONV_BLOCKS),
        proj_spec(off_xc, conv_lo, N_CONV_BLOCKS),
        proj_spec(off_u, sgu_lo, N_SGU_BLOCKS),
        proj_spec(off_v, sgu_lo, N_SGU_BLOCKS),
        pl.BlockSpec((None, 1, POOL_GROUP_DIM, POOL_GROUP_DIM),
                     lambda bi, j: (layer, _clamp(j, 0, N_POOL_BLOCKS), 0, 0)),
        pl.BlockSpec((None, 1, MIX_COLS),
                     lambda bi, j: (layer, 0, _clamp(j, 0, N_POOL_BLOCKS))),
        pl.BlockSpec((None, CONV_K, MIX_COLS),
                     lambda bi, j: (layer, 0, _clamp(j, conv_lo, N_CONV_BLOCKS))),
        pl.BlockSpec((None, 1, MIX_COLS),
                     lambda bi, j: (layer, 0, _clamp(j, sgu_lo, N_SGU_BLOCKS))),
        pl.BlockSpec((None, SGU_HEADS_PER_BLOCK, SGU_CHUNK, SGU_CHUNK),
                     lambda bi, j: (layer, _clamp(j, sgu_lo, N_SGU_BLOCKS), 0, 0)),
        pl.BlockSpec((None, SGU_HEADS_PER_BLOCK, SGU_CHUNK, 1),
                     lambda bi, j: (layer, _clamp(j, sgu_lo, N_SGU_BLOCKS), 0, 0)),
    ]
    return pl.pallas_call(
        _mixer_kernel,
        grid=(b, N_MIX_BLOCKS),
        in_specs=in_specs,
        out_specs=pl.BlockSpec((1, t, MIX_COLS), lambda bi, j: (bi, 0, j)),
        out_shape=jax.ShapeDtypeStruct((b, t, D_MODEL), BF16),
        compiler_params=_params("parallel", "arbitrary"),
        name="mixers",
    )(proj, proj, proj, proj, proj, proj,
      pool_w, pool_scale, conv_w, sgu_norm, sgu_w, sgu_b)


def kernel(x, ffn1_norm, ffn1_gate, ffn1_up, ffn1_down, mix_norm, w_in, pool_w, pool_scale,
           conv_w, sgu_norm, sgu_w, sgu_b, w_out, ffn2_norm, ffn2_gate, ffn2_up, ffn2_down,
           final_norm):
    b, t, d = x.shape
    m = b * t
    x = x.reshape(m, d)
    pool_w = pool_w.astype(BF16)
    pool_scale = pool_scale[:, None, :]
    sgu_norm = sgu_norm[:, None, :]
    sgu_b = sgu_b[..., None]

    xg, r = _norm_split(x, ffn1_norm[0])
    for l in range(DEPTH):
        act, wd = _ffn_up(xg, r, ffn1_gate, ffn1_up, ffn1_down, l)
        x, xg, r = _residual_matmul(act, wd, x, 0.5, mix_norm[l])

        proj, wo = _proj(xg, r, w_in, w_out, l)
        y = _mixers(proj.reshape(b, t, PROJ_WIDTH), pool_w, pool_scale, conv_w,
                    sgu_norm, sgu_w, sgu_b, l)
        x, xg, r = _residual_matmul(y.reshape(m, d), wo, x, 1.0, ffn2_norm[l])

        act, wd = _ffn_up(xg, r, ffn2_gate, ffn2_up, ffn2_down, l)
        next_gain = ffn1_norm[l + 1] if l + 1 < DEPTH else None
        x, xg, r = _residual_matmul(act, wd, x, 0.5, next_gain)
    return _rmsnorm(x, final_norm).reshape(b, t, d)
```

```python
import functools

import jax
import jax.numpy as jnp
from jax import lax
from jax.experimental import pallas as pl
from jax.experimental.pallas import tpu as pltpu

D_MODEL = 4096
D_FF = 5632
DEPTH = 4
POOL_GROUPS = 4
POOL_GROUP_DIM = 256
POOL_WIDTH = POOL_GROUPS * POOL_GROUP_DIM
POOL_WINDOWS = (2, 4, 8, 16)
CONV_WIDTH = 1536
CONV_K = 3
SGU_HEAD_DIM = 128
SGU_WIDTH = 1536
SGU_CHUNK = 128
PROJ_WIDTH = POOL_WIDTH + 3 * CONV_WIDTH + 2 * SGU_WIDTH
NORM_EPS = 1e-6
LN_EPS = 1e-5

F32 = jnp.float32
BF16 = jnp.bfloat16

VMEM_LIMIT_BYTES = 56 * 1024 * 1024
NORM_ROWS = 256
MM_ROWS = 1024
MM_COLS = 512
ROW_SPLITS = 4
MIX_COLS = 256
MIX_TIME = 512
BF16_SUBLANES = 16
POOL_HALO = 16
CONV_HALO = 16
SGU_UNROLL = 4

N_POOL_BLOCKS = POOL_WIDTH // MIX_COLS
N_CONV_BLOCKS = CONV_WIDTH // MIX_COLS
N_SGU_BLOCKS = SGU_WIDTH // MIX_COLS
N_MIX_BLOCKS = N_POOL_BLOCKS + N_CONV_BLOCKS + N_SGU_BLOCKS
SGU_HEADS_PER_BLOCK = MIX_COLS // SGU_HEAD_DIM


def _params(*semantics):
    return pltpu.CompilerParams(dimension_semantics=semantics,
                                vmem_limit_bytes=VMEM_LIMIT_BYTES)


def _row_scale(x):
    return lax.rsqrt(jnp.mean(x * x, axis=-1, keepdims=True) + NORM_EPS)


def _norm_split_kernel(x_ref, g_ref, xg_ref, r_ref):
    x = x_ref[...]
    xg_ref[...] = (x * g_ref[...]).astype(xg_ref.dtype)
    r_ref[...] = _row_scale(x)


def _norm_split(x, g):
    m, d = x.shape
    return pl.pallas_call(
        _norm_split_kernel,
        grid=(m // NORM_ROWS,),
        in_specs=[pl.BlockSpec((NORM_ROWS, d), lambda i: (i, 0)),
                  pl.BlockSpec((1, d), lambda i: (0, 0))],
        out_specs=[pl.BlockSpec((NORM_ROWS, d), lambda i: (i, 0)),
                   pl.BlockSpec((NORM_ROWS, 1), lambda i: (i, 0))],
        out_shape=[jax.ShapeDtypeStruct((m, d), BF16),
                   jax.ShapeDtypeStruct((m, 1), F32)],
        compiler_params=_params("parallel"),
        name="norm_split",
    )(x, g.reshape(1, d))


def _rmsnorm_kernel(x_ref, g_ref, o_ref):
    x = x_ref[...]
    o_ref[...] = x * _row_scale(x) * g_ref[...]


def _rmsnorm(x, g):
    m, d = x.shape
    return pl.pallas_call(
        _rmsnorm_kernel,
        grid=(m // NORM_ROWS,),
        in_specs=[pl.BlockSpec((NORM_ROWS, d), lambda i: (i, 0)),
                  pl.BlockSpec((1, d), lambda i: (0, 0))],
        out_specs=pl.BlockSpec((NORM_ROWS, d), lambda i: (i, 0)),
        out_shape=jax.ShapeDtypeStruct((m, d), F32),
        compiler_params=_params("parallel"),
        name="rmsnorm",
    )(x, g.reshape(1, d))


def _streamed_matmul_kernel(xg_ref, r_ref, *rest, n_weights, epilogue):
    w_refs = rest[:n_weights]
    side_in_ref, o_ref, side_out_ref = rest[n_weights:n_weights + 3]
    w_slots = rest[n_weights + 3:]
    jj = pl.program_id(0)
    chunk = w_refs[0].shape[0]
    rows = pl.ds(pl.multiple_of(pl.program_id(1) * chunk, chunk), chunk)

    def cast_chunk():
        for w_ref, w_s in zip(w_refs, w_slots):
            w_s[jj % 2, rows, :] = w_ref[...].astype(BF16)

    @pl.when(jj == 0)
    def _():
        cast_chunk()

    @pl.when(jj > 0)
    def _():
        cast_chunk()
        side_out_ref[...] = side_in_ref[...].astype(side_out_ref.dtype)
        xg = xg_ref[...]
        r = r_ref[...]
        cur = (jj - 1) % 2
        accs = [jnp.dot(xg, w_s[cur], preferred_element_type=F32) * r for w_s in w_slots]
        o_ref[...] = epilogue(*accs).astype(o_ref.dtype)


def _round_up(value, multiple):
    return -(-value // multiple) * multiple


def _streamed_matmul(xg, r, weights, side, layer, epilogue, out_dtype, name):
    m, k = xg.shape
    n = weights[0].shape[2]
    k2, n2 = side.shape[1:]
    rows = MM_ROWS
    n_row_tiles = m // rows
    n_col_tiles = n // MM_COLS
    chunk = k // n_row_tiles
    assert chunk * n_row_tiles == k and n_col_tiles * MM_COLS == n
    n_steps = n_col_tiles * n_row_tiles
    side_rows = _round_up(-(-k2 // n_steps), BF16_SUBLANES)
    n_side_blocks = k2 // side_rows
    assert n_side_blocks * side_rows == k2 and n_side_blocks <= n_steps

    def side_block(jj, i):
        return jnp.minimum(jnp.where(jj == 0, 0, (jj - 1) * n_row_tiles + i), n_side_blocks - 1)

    w_spec = pl.BlockSpec((None, chunk, MM_COLS),
                          lambda jj, i: (layer, i, jnp.minimum(jj, n_col_tiles - 1)))
    def row_tile(jj, i):
        return jnp.where(jj == 0, 0, i)

    out_spec = pl.BlockSpec((rows, MM_COLS),
                            lambda jj, i: (row_tile(jj, i), jnp.maximum(jj - 1, 0)))
    return pl.pallas_call(
        functools.partial(_streamed_matmul_kernel, n_weights=len(weights), epilogue=epilogue),
        grid=(n_col_tiles + 1, n_row_tiles),
        in_specs=([pl.BlockSpec((rows, k), lambda jj, i: (row_tile(jj, i), 0)),
                   pl.BlockSpec((rows, 1), lambda jj, i: (row_tile(jj, i), 0))]
                  + [w_spec] * len(weights)
                  + [pl.BlockSpec((None, side_rows, n2),
                                  lambda jj, i: (layer, side_block(jj, i), 0))]),
        out_specs=[out_spec,
                   pl.BlockSpec((side_rows, n2), lambda jj, i: (side_block(jj, i), 0))],
        out_shape=[jax.ShapeDtypeStruct((m, n), out_dtype),
                   jax.ShapeDtypeStruct((k2, n2), BF16)],
        scratch_shapes=[pltpu.VMEM((2, k, MM_COLS), BF16) for _ in weights],
        compiler_params=_params("arbitrary", "arbitrary"),
        name=name,
    )(xg, r, *weights, side)


def _swiglu_gate(g, u):
    h = 0.5 * g
    return (h + h * jnp.tanh(h)) * u


def _gelu_tanh(x):
    return 0.5 * x * (1.0 + jnp.tanh(0.7978845608028654 * (x + 0.044715 * (x * x * x))))


def _ffn_up(xg, r, wg, wu, wd, layer):
    return _streamed_matmul(xg, r, [wg, wu], wd, layer, _swiglu_gate, BF16, "ffn_up")


def _proj(xg, r, w, w_next, layer):
    return _streamed_matmul(xg, r, [w], w_next, layer, lambda acc: acc, BF16, "proj")


def _residual_matmul_kernel(a_ref, w_ref, x_ref, *rest, scale, emit_norm):
    if emit_norm:
        gn_ref, xo_ref, xg_ref, r_ref, ssq_s = rest
        j = pl.program_id(1)

        @pl.when(j == 0)
        def _():
            ssq_s[...] = jnp.zeros_like(ssq_s)
    else:
        (xo_ref,) = rest

    sub = a_ref.shape[0] // ROW_SPLITS
    for h in range(ROW_SPLITS):
        rows = slice(h * sub, (h + 1) * sub)
        acc = jnp.dot(a_ref[rows, :], w_ref[...], preferred_element_type=F32)
        xn = x_ref[rows, :] + scale * acc
        xo_ref[rows, :] = xn
        if emit_norm:
            xg_ref[rows, :] = (xn * gn_ref[...]).astype(xg_ref.dtype)
            ssq_s[rows, :] += jnp.sum(xn * xn, axis=-1, keepdims=True)

    if emit_norm:
        @pl.when(j == pl.num_programs(1) - 1)
        def _():
            r_ref[...] = lax.rsqrt(ssq_s[...] * (1.0 / D_MODEL) + NORM_EPS)


def _residual_matmul(a, w, x, scale, next_gain):
    m, k = a.shape
    n = w.shape[1]
    emit_norm = next_gain is not None
    tile_spec = pl.BlockSpec((MM_ROWS, MM_COLS), lambda i, j: (i, j))
    operands = [a, w, x]
    in_specs = [pl.BlockSpec((MM_ROWS, k), lambda i, j: (i, 0)),
                pl.BlockSpec((k, MM_COLS), lambda i, j: (0, j)),
                tile_spec]
    out_specs = [tile_spec]
    out_shape = [jax.ShapeDtypeStruct((m, n), F32)]
    scratch = []
    if emit_norm:
        assert n == D_MODEL
        operands.append(next_gain.reshape(1, n))
        in_specs.append(pl.BlockSpec((1, MM_COLS), lambda i, j: (0, j)))
        out_specs += [tile_spec, pl.BlockSpec((MM_ROWS, 1), lambda i, j: (i, 0))]
        out_shape += [jax.ShapeDtypeStruct((m, n), BF16),
                      jax.ShapeDtypeStruct((m, 1), F32)]
        scratch = [pltpu.VMEM((MM_ROWS, 1), F32)]
    outs = pl.pallas_call(
        functools.partial(_residual_matmul_kernel, scale=scale, emit_norm=emit_norm),
        grid=(m // MM_ROWS, n // MM_COLS),
        in_specs=in_specs,
        out_specs=out_specs,
        out_shape=out_shape,
        scratch_shapes=scratch,
        compiler_params=_params("parallel", "arbitrary"),
        name="residual_matmul",
    )(*operands)
    return outs if emit_norm else (outs[0], None, None)


def _time_chunks(ref, halo, n_time, chunk_fn):
    cols = ref.shape[-1]
    rows0 = jnp.concatenate([jnp.zeros((halo, cols), F32),
                             ref[0, 0:MIX_TIME, :].astype(F32)], axis=0)
    chunk_fn(0, rows0, True)

    def body(c, carry):
        t0 = pl.multiple_of(c * MIX_TIME, MIX_TIME)
        start = pl.multiple_of(t0 - halo, halo)
        chunk_fn(t0, ref[0, pl.ds(start, MIX_TIME + halo), :].astype(F32), False)
        return carry

    lax.fori_loop(1, n_time // MIX_TIME, body, 0)


def _pool_block(a_ref, pw_ref, ps_ref, o_ref, window):
    n_time = a_ref.shape[1]
    w = pw_ref[0]
    scale = ps_ref[...]

    def chunk(t0, rows, first):
        s = rows
        shift = 1
        while shift < window:
            s = s + pltpu.roll(s, shift, axis=0)
            shift *= 2
        if first:
            t = lax.broadcasted_iota(jnp.int32, (MIX_TIME, 1), 0)
            pooled = s[POOL_HALO:] / jnp.minimum(t + 1, window).astype(F32)
        else:
            pooled = s[POOL_HALO:] * (1.0 / window)
        pa = pooled - rows[POOL_HALO:]
        y = jnp.dot(pa.astype(BF16), w, preferred_element_type=F32) * scale
        o_ref[0, pl.ds(t0, MIX_TIME), :] = y.astype(o_ref.dtype)

    _time_chunks(a_ref, POOL_HALO, n_time, chunk)


def _conv_block(gb_ref, gc_ref, xc_ref, cw_ref, o_ref):
    n_time = gb_ref.shape[1]
    w0 = cw_ref[0:1, :]
    w1 = cw_ref[1:2, :]
    w2 = cw_ref[2:3, :]
    cols = gb_ref.shape[-1]

    def chunk_rows(ref, t0):
        start = pl.multiple_of(t0 - CONV_HALO, CONV_HALO)
        return ref[0, pl.ds(start, MIX_TIME + CONV_HALO), :].astype(F32)

    def compute(t0, z):
        conv = w2 * z + w1 * pltpu.roll(z, 1, axis=0) + w0 * pltpu.roll(z, 2, axis=0)
        y = gb_ref[0, pl.ds(t0, MIX_TIME), :].astype(F32) * conv[CONV_HALO:]
        o_ref[0, pl.ds(t0, MIX_TIME), :] = y.astype(o_ref.dtype)

    z_first = gc_ref[0, 0:MIX_TIME, :].astype(F32) * xc_ref[0, 0:MIX_TIME, :].astype(F32)
    compute(0, jnp.concatenate([jnp.zeros((CONV_HALO, cols), F32), z_first], axis=0))

    def body(c, carry):
        t0 = pl.multiple_of(c * MIX_TIME, MIX_TIME)
        compute(t0, chunk_rows(gc_ref, t0) * chunk_rows(xc_ref, t0))
        return carry

    lax.fori_loop(1, n_time // MIX_TIME, body, 0)


def _sgu_block(u_ref, v_ref, lg_ref, ws_ref, bs_ref, o_ref):
    n_time = u_ref.shape[1]
    row = lax.broadcasted_iota(jnp.int32, (SGU_CHUNK, SGU_CHUNK), 0)
    col = lax.broadcasted_iota(jnp.int32, (SGU_CHUNK, SGU_CHUNK), 1)
    causal = row >= col
    heads = []
    for hh in range(SGU_HEADS_PER_BLOCK):
        lanes = slice(hh * SGU_HEAD_DIM, (hh + 1) * SGU_HEAD_DIM)
        ws = jnp.where(causal, ws_ref[hh], 0.0).astype(BF16)
        heads.append((lanes, ws, bs_ref[hh], lg_ref[:, lanes]))

    def body(n, carry):
        t0 = pl.multiple_of(n * SGU_CHUNK, SGU_CHUNK)
        for lanes, ws, bias, gain in heads:
            v = _gelu_tanh(v_ref[0, pl.ds(t0, SGU_CHUNK), lanes].astype(F32))
            mu = jnp.mean(v, axis=-1, keepdims=True)
            vc = v - mu
            var = jnp.mean(vc * vc, axis=-1, keepdims=True)
            vn = vc * lax.rsqrt(var + LN_EPS) * gain
            mixed = jnp.dot(ws, vn.astype(BF16), preferred_element_type=F32) + bias
            u = _gelu_tanh(u_ref[0, pl.ds(t0, SGU_CHUNK), lanes].astype(F32))
            o_ref[0, pl.ds(t0, SGU_CHUNK), lanes] = (u * mixed).astype(o_ref.dtype)
        return carry

    lax.fori_loop(0, n_time // SGU_CHUNK, body, 0, unroll=SGU_UNROLL)


def _mixer_kernel(a_ref, gb_ref, gc_ref, xc_ref, u_ref, v_ref,
                  pw_ref, ps_ref, cw_ref, lg_ref, ws_ref, bs_ref, o_ref):
    j = pl.program_id(1)
    for g, window in enumerate(POOL_WINDOWS):
        @pl.when(j == g)
        def _(window=window):
            _pool_block(a_ref, pw_ref, ps_ref, o_ref, window)

    @pl.when(jnp.logical_and(j >= N_POOL_BLOCKS, j < N_POOL_BLOCKS + N_CONV_BLOCKS))
    def _():
        _conv_block(gb_ref, gc_ref, xc_ref, cw_ref, o_ref)

    @pl.when(j >= N_POOL_BLOCKS + N_CONV_BLOCKS)
    def _():
        _sgu_block(u_ref, v_ref, lg_ref, ws_ref, bs_ref, o_ref)


def _clamp(j, lo, n):
    return jnp.clip(j - lo, 0, n - 1)


def _mixers(proj, pool_w, pool_scale, conv_w, sgu_norm, sgu_w, sgu_b, layer):
    b, t, _ = proj.shape
    conv_lo = N_POOL_BLOCKS
    sgu_lo = N_POOL_BLOCKS + N_CONV_BLOCKS
    off_gb = POOL_WIDTH // MIX_COLS
    off_gc = off_gb + N_CONV_BLOCKS
    off_xc = off_gc + N_CONV_BLOCKS
    off_u = off_xc + N_CONV_BLOCKS
    off_v = off_u + N_SGU_BLOCKS

    def proj_spec(offset, lo, n):
        return pl.BlockSpec((1, t, MIX_COLS), lambda bi, j: (bi, 0, offset + _clamp(j, lo, n)))

    in_specs = [
        proj_spec(0, 0, N_POOL_BLOCKS),
        proj_spec(off_gb, conv_lo, N_CONV_BLOCKS),
        proj_spec(off_gc, conv_lo, N_CONV_BLOCKS),
        proj_spec(off_xc, conv_lo, N_CONV_BLOCKS),
        proj_spec(off_u, sgu_lo, N_SGU_BLOCKS),
        proj_spec(off_v, sgu_lo, N_SGU_BLOCKS),
        pl.BlockSpec((None, 1, POOL_GROUP_DIM, POOL_GROUP_DIM),
                     lambda bi, j: (layer, _clamp(j, 0, N_POOL_BLOCKS), 0, 0)),
        pl.BlockSpec((None, 1, MIX_COLS),
                     lambda bi, j: (layer, 0, _clamp(j, 0, N_POOL_BLOCKS))),
        pl.BlockSpec((None, CONV_K, MIX_COLS),
                     lambda bi, j: (layer, 0, _clamp(j, conv_lo, N_CONV_BLOCKS))),
        pl.BlockSpec((None, 1, MIX_COLS),
                     lambda bi, j: (layer, 0, _clamp(j, sgu_lo, N_SGU_BLOCKS))),
        pl.BlockSpec((None, SGU_HEADS_PER_BLOCK, SGU_CHUNK, SGU_CHUNK),
                     lambda bi, j: (layer, _clamp(j, sgu_lo, N_SGU_BLOCKS), 0, 0)),
        pl.BlockSpec((None, SGU_HEADS_PER_BLOCK, SGU_CHUNK, 1),
                     lambda bi, j: (layer, _clamp(j, sgu_lo, N_SGU_BLOCKS), 0, 0)),
    ]
    return pl.pallas_call(
        _mixer_kernel,
        grid=(b, N_MIX_BLOCKS),
        in_specs=in_specs,
        out_specs=pl.BlockSpec((1, t, MIX_COLS), lambda bi, j: (bi, 0, j)),
        out_shape=jax.ShapeDtypeStruct((b, t, D_MODEL), BF16),
        compiler_params=_params("parallel", "arbitrary"),
        name="mixers",
    )(proj, proj, proj, proj, proj, proj,
      pool_w, pool_scale, conv_w, sgu_norm, sgu_w, sgu_b)


def kernel(x, ffn1_norm, ffn1_gate, ffn1_up, ffn1_down, mix_norm, w_in, pool_w, pool_scale,
           conv_w, sgu_norm, sgu_w, sgu_b, w_out, ffn2_norm, ffn2_gate, ffn2_up, ffn2_down,
           final_norm):
    b, t, d = x.shape
    m = b * t
    x = x.reshape(m, d)
    pool_w = pool_w.astype(BF16)
    pool_scale = pool_scale[:, None, :]
    sgu_norm = sgu_norm[:, None, :]
    sgu_b = sgu_b[..., None]

    xg, r = _norm_split(x, ffn1_norm[0])
    for l in range(DEPTH):
        act, wd = _ffn_up(xg, r, ffn1_gate, ffn1_up, ffn1_down, l)
        x, xg, r = _residual_matmul(act, wd, x, 0.5, mix_norm[l])

        proj, wo = _proj(xg, r, w_in, w_out, l)
        y = _mixers(proj.reshape(b, t, PROJ_WIDTH), pool_w, pool_scale, conv_w,
                    sgu_norm, sgu_w, sgu_b, l)
        x, xg, r = _residual_matmul(y.reshape(m, d), wo, x, 1.0, ffn2_norm[l])

        act, wd = _ffn_up(xg, r, ffn2_gate, ffn2_up, ffn2_down, l)
        next_gain = ffn1_norm[l + 1] if l + 1 < DEPTH else None
        x, xg, r = _residual_matmul(act, wd, x, 0.5, next_gain)
    return _rmsnorm(x, final_norm).reshape(b, t, d)
```

```python
import functools

import jax
import jax.numpy as jnp
from jax import lax
from jax.experimental import pallas as pl
from jax.experimental.pallas import tpu as pltpu

D_MODEL = 4096
D_FF = 5632
DEPTH = 4
POOL_GROUPS = 4
POOL_GROUP_DIM = 256
POOL_WIDTH = POOL_GROUPS * POOL_GROUP_DIM
POOL_WINDOWS = (2, 4, 8, 16)
CONV_WIDTH = 1536
CONV_K = 3
SGU_HEAD_DIM = 128
SGU_WIDTH = 1536
SGU_CHUNK = 128
PROJ_WIDTH = POOL_WIDTH + 3 * CONV_WIDTH + 2 * SGU_WIDTH
NORM_EPS = 1e-6
LN_EPS = 1e-5

F32 = jnp.float32
BF16 = jnp.bfloat16

VMEM_LIMIT_BYTES = 56 * 1024 * 1024
NORM_ROWS = 256
MM_ROWS = 1024
MM_COLS = 512
ROW_SPLITS = 4
MIX_COLS = 256
MIX_TIME = 1024
BF16_SUBLANES = 16
POOL_HALO = 16
CONV_HALO = 16
SGU_UNROLL = 4

N_POOL_BLOCKS = POOL_WIDTH // MIX_COLS
N_CONV_BLOCKS = CONV_WIDTH // MIX_COLS
N_SGU_BLOCKS = SGU_WIDTH // MIX_COLS
N_MIX_BLOCKS = N_POOL_BLOCKS + N_CONV_BLOCKS + N_SGU_BLOCKS
SGU_HEADS_PER_BLOCK = MIX_COLS // SGU_HEAD_DIM


def _params(*semantics):
    return pltpu.CompilerParams(dimension_semantics=semantics,
                                vmem_limit_bytes=VMEM_LIMIT_BYTES)


def _row_scale(x):
    return lax.rsqrt(jnp.mean(x * x, axis=-1, keepdims=True) + NORM_EPS)


def _norm_split_kernel(x_ref, g_ref, xg_ref, r_ref):
    x = x_ref[...]
    xg_ref[...] = (x * g_ref[...]).astype(xg_ref.dtype)
    r_ref[...] = _row_scale(x)


def _norm_split(x, g):
    m, d = x.shape
    return pl.pallas_call(
        _norm_split_kernel,
        grid=(m // NORM_ROWS,),
        in_specs=[pl.BlockSpec((NORM_ROWS, d), lambda i: (i, 0)),
                  pl.BlockSpec((1, d), lambda i: (0, 0))],
        out_specs=[pl.BlockSpec((NORM_ROWS, d), lambda i: (i, 0)),
                   pl.BlockSpec((NORM_ROWS, 1), lambda i: (i, 0))],
        out_shape=[jax.ShapeDtypeStruct((m, d), BF16),
                   jax.ShapeDtypeStruct((m, 1), F32)],
        compiler_params=_params("parallel"),
        name="norm_split",
    )(x, g.reshape(1, d))


def _rmsnorm_kernel(x_ref, g_ref, o_ref):
    x = x_ref[...]
    o_ref[...] = x * _row_scale(x) * g_ref[...]


def _rmsnorm(x, g):
    m, d = x.shape
    return pl.pallas_call(
        _rmsnorm_kernel,
        grid=(m // NORM_ROWS,),
        in_specs=[pl.BlockSpec((NORM_ROWS, d), lambda i: (i, 0)),
                  pl.BlockSpec((1, d), lambda i: (0, 0))],
        out_specs=pl.BlockSpec((NORM_ROWS, d), lambda i: (i, 0)),
        out_shape=jax.ShapeDtypeStruct((m, d), F32),
        compiler_params=_params("parallel"),
        name="rmsnorm",
    )(x, g.reshape(1, d))


def _streamed_matmul_kernel(xg_ref, r_ref, *rest, n_weights, epilogue):
    w_refs = rest[:n_weights]
    side_in_ref, o_ref, side_out_ref = rest[n_weights:n_weights + 3]
    w_slots = rest[n_weights + 3:]
    jj = pl.program_id(0)
    chunk = w_refs[0].shape[0]
    rows = pl.ds(pl.multiple_of(pl.program_id(1) * chunk, chunk), chunk)

    def cast_chunk():
        for w_ref, w_s in zip(w_refs, w_slots):
            w_s[jj % 2, rows, :] = w_ref[...].astype(BF16)

    @pl.when(jj == 0)
    def _():
        cast_chunk()

    @pl.when(jj > 0)
    def _():
        cast_chunk()
        side_out_ref[...] = side_in_ref[...].astype(side_out_ref.dtype)
        xg = xg_ref[...]
        r = r_ref[...]
        cur = (jj - 1) % 2
        accs = [jnp.dot(xg, w_s[cur], preferred_element_type=F32) * r for w_s in w_slots]
        o_ref[...] = epilogue(*accs).astype(o_ref.dtype)


def _round_up(value, multiple):
    return -(-value // multiple) * multiple


def _streamed_matmul(xg, r, weights, side, layer, epilogue, out_dtype, name):
    m, k = xg.shape
    n = weights[0].shape[2]
    k2, n2 = side.shape[1:]
    rows = MM_ROWS
    n_row_tiles = m // rows
    n_col_tiles = n // MM_COLS
    chunk = k // n_row_tiles
    assert chunk * n_row_tiles == k and n_col_tiles * MM_COLS == n
    n_steps = n_col_tiles * n_row_tiles
    side_rows = _round_up(-(-k2 // n_steps), BF16_SUBLANES)
    n_side_blocks = k2 // side_rows
    assert n_side_blocks * side_rows == k2 and n_side_blocks <= n_steps

    def side_block(jj, i):
        return jnp.minimum(jnp.where(jj == 0, 0, (jj - 1) * n_row_tiles + i), n_side_blocks - 1)

    w_spec = pl.BlockSpec((None, chunk, MM_COLS),
                          lambda jj, i: (layer, i, jnp.minimum(jj, n_col_tiles - 1)))
    def row_tile(jj, i):
        return jnp.where(jj == 0, 0, i)

    out_spec = pl.BlockSpec((rows, MM_COLS),
                            lambda jj, i: (row_tile(jj, i), jnp.maximum(jj - 1, 0)))
    return pl.pallas_call(
        functools.partial(_streamed_matmul_kernel, n_weights=len(weights), epilogue=epilogue),
        grid=(n_col_tiles + 1, n_row_tiles),
        in_specs=([pl.BlockSpec((rows, k), lambda jj, i: (row_tile(jj, i), 0)),
                   pl.BlockSpec((rows, 1), lambda jj, i: (row_tile(jj, i), 0))]
                  + [w_spec] * len(weights)
                  + [pl.BlockSpec((None, side_rows, n2),
                                  lambda jj, i: (layer, side_block(jj, i), 0))]),
        out_specs=[out_spec,
                   pl.BlockSpec((side_rows, n2), lambda jj, i: (side_block(jj, i), 0))],
        out_shape=[jax.ShapeDtypeStruct((m, n), out_dtype),
                   jax.ShapeDtypeStruct((k2, n2), BF16)],
        scratch_shapes=[pltpu.VMEM((2, k, MM_COLS), BF16) for _ in weights],
        compiler_params=_params("arbitrary", "arbitrary"),
        name=name,
    )(xg, r, *weights, side)


def _swiglu_gate(g, u):
    h = 0.5 * g
    return (h + h * jnp.tanh(h)) * u


def _gelu_tanh(x):
    return 0.5 * x * (1.0 + jnp.tanh(0.7978845608028654 * (x + 0.044715 * (x * x * x))))


def _ffn_up(xg, r, wg, wu, wd, layer):
    return _streamed_matmul(xg, r, [wg, wu], wd, layer, _swiglu_gate, BF16, "ffn_up")


def _proj(xg, r, w, w_next, layer):
    return _streamed_matmul(xg, r, [w], w_next, layer, lambda acc: acc, BF16, "proj")


def _residual_matmul_kernel(a_ref, w_ref, x_ref, *rest, scale, emit_norm):
    if emit_norm:
        gn_ref, xo_ref, xg_ref, r_ref, ssq_s = rest
        j = pl.program_id(1)

        @pl.when(j == 0)
        def _():
            ssq_s[...] = jnp.zeros_like(ssq_s)
    else:
        (xo_ref,) = rest

    sub = a_ref.shape[0] // ROW_SPLITS
    for h in range(ROW_SPLITS):
        rows = slice(h * sub, (h + 1) * sub)
        acc = jnp.dot(a_ref[rows, :], w_ref[...], preferred_element_type=F32)
        xn = x_ref[rows, :] + scale * acc
        xo_ref[rows, :] = xn
        if emit_norm:
            xg_ref[rows, :] = (xn * gn_ref[...]).astype(xg_ref.dtype)
            ssq_s[rows, :] += jnp.sum(xn * xn, axis=-1, keepdims=True)

    if emit_norm:
        @pl.when(j == pl.num_programs(1) - 1)
        def _():
            r_ref[...] = lax.rsqrt(ssq_s[...] * (1.0 / D_MODEL) + NORM_EPS)


def _residual_matmul(a, w, x, scale, next_gain):
    m, k = a.shape
    n = w.shape[1]
    emit_norm = next_gain is not None
    tile_spec = pl.BlockSpec((MM_ROWS, MM_COLS), lambda i, j: (i, j))
    operands = [a, w, x]
    in_specs = [pl.BlockSpec((MM_ROWS, k), lambda i, j: (i, 0)),
                pl.BlockSpec((k, MM_COLS), lambda i, j: (0, j)),
                tile_spec]
    out_specs = [tile_spec]
    out_shape = [jax.ShapeDtypeStruct((m, n), F32)]
    scratch = []
    if emit_norm:
        assert n == D_MODEL
        operands.append(next_gain.reshape(1, n))
        in_specs.append(pl.BlockSpec((1, MM_COLS), lambda i, j: (0, j)))
        out_specs += [tile_spec, pl.BlockSpec((MM_ROWS, 1), lambda i, j: (i, 0))]
        out_shape += [jax.ShapeDtypeStruct((m, n), BF16),
                      jax.ShapeDtypeStruct((m, 1), F32)]
        scratch = [pltpu.VMEM((MM_ROWS, 1), F32)]
    outs = pl.pallas_call(
        functools.partial(_residual_matmul_kernel, scale=scale, emit_norm=emit_norm),
        grid=(m // MM_ROWS, n // MM_COLS),
        in_specs=in_specs,
        out_specs=out_specs,
        out_shape=out_shape,
        scratch_shapes=scratch,
        compiler_params=_params("parallel", "arbitrary"),
        name="residual_matmul",
    )(*operands)
    return outs if emit_norm else (outs[0], None, None)


def _time_chunks(ref, halo, n_time, chunk_fn):
    cols = ref.shape[-1]
    rows0 = jnp.concatenate([jnp.zeros((halo, cols), F32),
                             ref[0, 0:MIX_TIME, :].astype(F32)], axis=0)
    chunk_fn(0, rows0, True)

    def body(c, carry):
        t0 = pl.multiple_of(c * MIX_TIME, MIX_TIME)
        start = pl.multiple_of(t0 - halo, halo)
        chunk_fn(t0, ref[0, pl.ds(start, MIX_TIME + halo), :].astype(F32), False)
        return carry

    lax.fori_loop(1, n_time // MIX_TIME, body, 0)


def _pool_block(a_ref, pw_ref, ps_ref, o_ref, window):
    n_time = a_ref.shape[1]
    w = pw_ref[0]
    scale = ps_ref[...]

    def chunk(t0, rows, first):
        s = rows
        shift = 1
        while shift < window:
            s = s + pltpu.roll(s, shift, axis=0)
            shift *= 2
        if first:
            t = lax.broadcasted_iota(jnp.int32, (MIX_TIME, 1), 0)
            pooled = s[POOL_HALO:] / jnp.minimum(t + 1, window).astype(F32)
        else:
            pooled = s[POOL_HALO:] * (1.0 / window)
        pa = pooled - rows[POOL_HALO:]
        y = jnp.dot(pa.astype(BF16), w, preferred_element_type=F32) * scale
        o_ref[0, pl.ds(t0, MIX_TIME), :] = y.astype(o_ref.dtype)

    _time_chunks(a_ref, POOL_HALO, n_time, chunk)


def _conv_block(gb_ref, gc_ref, xc_ref, cw_ref, o_ref):
    n_time = gb_ref.shape[1]
    w0 = cw_ref[0:1, :]
    w1 = cw_ref[1:2, :]
    w2 = cw_ref[2:3, :]
    cols = gb_ref.shape[-1]

    def chunk_rows(ref, t0):
        start = pl.multiple_of(t0 - CONV_HALO, CONV_HALO)
        return ref[0, pl.ds(start, MIX_TIME + CONV_HALO), :].astype(F32)

    def compute(t0, z):
        conv = w2 * z + w1 * pltpu.roll(z, 1, axis=0) + w0 * pltpu.roll(z, 2, axis=0)
        y = gb_ref[0, pl.ds(t0, MIX_TIME), :].astype(F32) * conv[CONV_HALO:]
        o_ref[0, pl.ds(t0, MIX_TIME), :] = y.astype(o_ref.dtype)

    z_first = gc_ref[0, 0:MIX_TIME, :].astype(F32) * xc_ref[0, 0:MIX_TIME, :].astype(F32)
    compute(0, jnp.concatenate([jnp.zeros((CONV_HALO, cols), F32), z_first], axis=0))

    def body(c, carry):
        t0 = pl.multiple_of(c * MIX_TIME, MIX_TIME)
        compute(t0, chunk_rows(gc_ref, t0) * chunk_rows(xc_ref, t0))
        return carry

    lax.fori_loop(1, n_time // MIX_TIME, body, 0)


def _sgu_block(u_ref, v_ref, lg_ref, ws_ref, bs_ref, o_ref):
    n_time = u_ref.shape[1]
    row = lax.broadcasted_iota(jnp.int32, (SGU_CHUNK, SGU_CHUNK), 0)
    col = lax.broadcasted_iota(jnp.int32, (SGU_CHUNK, SGU_CHUNK), 1)
    causal = row >= col
    heads = []
    for hh in range(SGU_HEADS_PER_BLOCK):
        lanes = slice(hh * SGU_HEAD_DIM, (hh + 1) * SGU_HEAD_DIM)
        ws = jnp.where(causal, ws_ref[hh], 0.0).astype(BF16)
        heads.append((lanes, ws, bs_ref[hh], lg_ref[:, lanes]))

    def body(n, carry):
        t0 = pl.multiple_of(n * SGU_CHUNK, SGU_CHUNK)
        for lanes, ws, bias, gain in heads:
            v = _gelu_tanh(v_ref[0, pl.ds(t0, SGU_CHUNK), lanes].astype(F32))
            mu = jnp.mean(v, axis=-1, keepdims=True)
            vc = v - mu
            var = jnp.mean(vc * vc, axis=-1, keepdims=True)
            vn = vc * lax.rsqrt(var + LN_EPS) * gain
            mixed = jnp.dot(ws, vn.astype(BF16), preferred_element_type=F32) + bias
            u = _gelu_tanh(u_ref[0, pl.ds(t0, SGU_CHUNK), lanes].astype(F32))
            o_ref[0, pl.ds(t0, SGU_CHUNK), lanes] = (u * mixed).astype(o_ref.dtype)
        return carry

    lax.fori_loop(0, n_time // SGU_CHUNK, body, 0, unroll=SGU_UNROLL)


def _mixer_kernel(a_ref, gb_ref, gc_ref, xc_ref, u_ref, v_ref,
                  pw_ref, ps_ref, cw_ref, lg_ref, ws_ref, bs_ref, o_ref):
    j = pl.program_id(1)
    for g, window in enumerate(POOL_WINDOWS):
        @pl.when(j == g)
        def _(window=window):
            _pool_block(a_ref, pw_ref, ps_ref, o_ref, window)

    @pl.when(jnp.logical_and(j >= N_POOL_BLOCKS, j < N_POOL_BLOCKS + N_CONV_BLOCKS))
    def _():
        _conv_block(gb_ref, gc_ref, xc_ref, cw_ref, o_ref)

    @pl.when(j >= N_POOL_BLOCKS + N_CONV_BLOCKS)
    def _():
        _sgu_block(u_ref, v_ref, lg_ref, ws_ref, bs_ref, o_ref)


def _clamp(j, lo, n):
    return jnp.clip(j - lo, 0, n - 1)


def _mixers(proj, pool_w, pool_scale, conv_w, sgu_norm, sgu_w, sgu_b, layer):
    b, t, _ = proj.shape
    conv_lo = N_POOL_BLOCKS
    sgu_lo = N_POOL_BLOCKS + N_CONV_BLOCKS
    off_gb = POOL_WIDTH // MIX_COLS
    off_gc = off_gb + N_CONV_BLOCKS
    off_xc = off_gc + N_CONV_BLOCKS
    off_u = off_xc + N_CONV_BLOCKS
    off_v = off_u + N_SGU_BLOCKS

    def proj_spec(offset, lo, n):
        return pl.BlockSpec((1, t, MIX_COLS), lambda bi, j: (bi, 0, offset + _clamp(j, lo, n)))

    in_specs = [
        proj_spec(0, 0, N_POOL_BLOCKS),
        proj_spec(off_gb, conv_lo, N_CONV_BLOCKS),
        proj_spec(off_gc, conv_lo, N_CONV_BLOCKS),
        proj_spec(off_xc, conv_lo, N_CONV_BLOCKS),
        proj_spec(off_u, sgu_lo, N_SGU_BLOCKS),
        proj_spec(off_v, sgu_lo, N_SGU_BLOCKS),
        pl.BlockSpec((None, 1, POOL_GROUP_DIM, POOL_GROUP_DIM),
                     lambda bi, j: (layer, _clamp(j, 0, N_POOL_BLOCKS), 0, 0)),
        pl.BlockSpec((None, 1, MIX_COLS),
                     lambda bi, j: (layer, 0, _clamp(j, 0, N_POOL_BLOCKS))),
        pl.BlockSpec((None, CONV_K, MIX_COLS),
                     lambda bi, j: (layer, 0, _clamp(j, conv_lo, N_CONV_BLOCKS))),
        pl.BlockSpec((None, 1, MIX_COLS),
                     lambda bi, j: (layer, 0, _clamp(j, sgu_lo, N_SGU_BLOCKS))),
        pl.BlockSpec((None, SGU_HEADS_PER_BLOCK, SGU_CHUNK, SGU_CHUNK),
                     lambda bi, j: (layer, _clamp(j, sgu_lo, N_SGU_BLOCKS), 0, 0)),
        pl.BlockSpec((None, SGU_HEADS_PER_BLOCK, SGU_CHUNK, 1),
                     lambda bi, j: (layer, _clamp(j, sgu_lo, N_SGU_BLOCKS), 0, 0)),
    ]
    return pl.pallas_call(
        _mixer_kernel,
        grid=(b, N_MIX_BLOCKS),
        in_specs=in_specs,
        out_specs=pl.BlockSpec((1, t, MIX_COLS), lambda bi, j: (bi, 0, j)),
        out_shape=jax.ShapeDtypeStruct((b, t, D_MODEL), BF16),
        compiler_params=_params("parallel", "arbitrary"),
        name="mixers",
    )(proj, proj, proj, proj, proj, proj,
      pool_w, pool_scale, conv_w, sgu_norm, sgu_w, sgu_b)


def kernel(x, ffn1_norm, ffn1_gate, ffn1_up, ffn1_down, mix_norm, w_in, pool_w, pool_scale,
           conv_w, sgu_norm, sgu_w, sgu_b, w_out, ffn2_norm, ffn2_gate, ffn2_up, ffn2_down,
           final_norm):
    b, t, d = x.shape
    m = b * t
    x = x.reshape(m, d)
    pool_w = pool_w.astype(BF16)
    pool_scale = pool_scale[:, None, :]
    sgu_norm = sgu_norm[:, None, :]
    sgu_b = sgu_b[..., None]

    xg, r = _norm_split(x, ffn1_norm[0])
    for l in range(DEPTH):
        act, wd = _ffn_up(xg, r, ffn1_gate, ffn1_up, ffn1_down, l)
        x, xg, r = _residual_matmul(act, wd, x, 0.5, mix_norm[l])

        proj, wo = _proj(xg, r, w_in, w_out, l)
        y = _mixers(proj.reshape(b, t, PROJ_WIDTH), pool_w, pool_scale, conv_w,
                    sgu_norm, sgu_w, sgu_b, l)
        x, xg, r = _residual_matmul(y.reshape(m, d), wo, x, 1.0, ffn2_norm[l])

        act, wd = _ffn_up(xg, r, ffn2_gate, ffn2_up, ffn2_down, l)
        next_gain = ffn1_norm[l + 1] if l + 1 < DEPTH else None
        x, xg, r = _residual_matmul(act, wd, x, 0.5, next_gain)
    return _rmsnorm(x, final_norm).reshape(b, t, d)
```

```python
import functools

import jax
import jax.numpy as jnp
from jax import lax
from jax.experimental import pallas as pl
from jax.experimental.pallas import tpu as pltpu

D_MODEL = 4096
D_FF = 5632
DEPTH = 4
POOL_GROUPS = 4
POOL_GROUP_DIM = 256
POOL_WIDTH = POOL_GROUPS * POOL_GROUP_DIM
POOL_WINDOWS = (2, 4, 8, 16)
CONV_WIDTH = 1536
CONV_K = 3
SGU_HEAD_DIM = 128
SGU_WIDTH = 1536
SGU_CHUNK = 128
PROJ_WIDTH = POOL_WIDTH + 3 * CONV_WIDTH + 2 * SGU_WIDTH
NORM_EPS = 1e-6
LN_EPS = 1e-5

F32 = jnp.float32
BF16 = jnp.bfloat16

VMEM_LIMIT_BYTES = 56 * 1024 * 1024
NORM_ROWS = 256
MM_ROWS = 1024
MM_COLS = 512
ROW_SPLITS = 4
MIX_COLS = 256
MIX_TIME = 1024
BF16_SUBLANES = 16
POOL_HALO = 16
CONV_HALO = 16
SGU_UNROLL = 4

N_POOL_BLOCKS = POOL_WIDTH // MIX_COLS
N_CONV_BLOCKS = CONV_WIDTH // MIX_COLS
N_SGU_BLOCKS = SGU_WIDTH // MIX_COLS
N_MIX_BLOCKS = N_POOL_BLOCKS + N_CONV_BLOCKS + N_SGU_BLOCKS
SGU_HEADS_PER_BLOCK = MIX_COLS // SGU_HEAD_DIM


def _params(*semantics):
    return pltpu.CompilerParams(dimension_semantics=semantics,
                                vmem_limit_bytes=VMEM_LIMIT_BYTES)


def _row_scale(x):
    return lax.rsqrt(jnp.mean(x * x, axis=-1, keepdims=True) + NORM_EPS)


def _norm_split_kernel(x_ref, g_ref, xg_ref, r_ref):
    x = x_ref[...]
    xg_ref[...] = (x * g_ref[...]).astype(xg_ref.dtype)
    r_ref[...] = _row_scale(x)


def _norm_split(x, g):
    m, d = x.shape
    return pl.pallas_call(
        _norm_split_kernel,
        grid=(m // NORM_ROWS,),
        in_specs=[pl.BlockSpec((NORM_ROWS, d), lambda i: (i, 0)),
                  pl.BlockSpec((1, d), lambda i: (0, 0))],
        out_specs=[pl.BlockSpec((NORM_ROWS, d), lambda i: (i, 0)),
                   pl.BlockSpec((NORM_ROWS, 1), lambda i: (i, 0))],
        out_shape=[jax.ShapeDtypeStruct((m, d), BF16),
                   jax.ShapeDtypeStruct((m, 1), F32)],
        compiler_params=_params("parallel"),
        name="norm_split",
    )(x, g.reshape(1, d))


def _rmsnorm_kernel(x_ref, g_ref, o_ref):
    x = x_ref[...]
    o_ref[...] = x * _row_scale(x) * g_ref[...]


def _rmsnorm(x, g):
    m, d = x.shape
    return pl.pallas_call(
        _rmsnorm_kernel,
        grid=(m // NORM_ROWS,),
        in_specs=[pl.BlockSpec((NORM_ROWS, d), lambda i: (i, 0)),
                  pl.BlockSpec((1, d), lambda i: (0, 0))],
        out_specs=pl.BlockSpec((NORM_ROWS, d), lambda i: (i, 0)),
        out_shape=jax.ShapeDtypeStruct((m, d), F32),
        compiler_params=_params("parallel"),
        name="rmsnorm",
    )(x, g.reshape(1, d))


def _streamed_matmul_kernel(xg_ref, r_ref, *rest, n_weights, epilogue):
    w_refs = rest[:n_weights]
    side_in_ref, o_ref, side_out_ref = rest[n_weights:n_weights + 3]
    w_slots = rest[n_weights + 3:]
    jj = pl.program_id(0)
    chunk = w_refs[0].shape[0]
    rows = pl.ds(pl.multiple_of(pl.program_id(1) * chunk, chunk), chunk)

    def cast_chunk():
        for w_ref, w_s in zip(w_refs, w_slots):
            w_s[jj % 2, rows, :] = w_ref[...].astype(BF16)

    @pl.when(jj == 0)
    def _():
        cast_chunk()

    @pl.when(jj > 0)
    def _():
        cast_chunk()
        side_out_ref[...] = side_in_ref[...].astype(side_out_ref.dtype)
        xg = xg_ref[...]
        r = r_ref[...]
        cur = (jj - 1) % 2
        accs = [jnp.dot(xg, w_s[cur], preferred_element_type=F32) * r for w_s in w_slots]
        o_ref[...] = epilogue(*accs).astype(o_ref.dtype)


def _round_up(value, multiple):
    return -(-value // multiple) * multiple


def _streamed_matmul(xg, r, weights, side, layer, epilogue, out_dtype, name):
    m, k = xg.shape
    n = weights[0].shape[2]
    k2, n2 = side.shape[1:]
    rows = MM_ROWS
    n_row_tiles = m // rows
    n_col_tiles = n // MM_COLS
    chunk = k // n_row_tiles
    assert chunk * n_row_tiles == k and n_col_tiles * MM_COLS == n
    n_steps = n_col_tiles * n_row_tiles
    side_rows = _round_up(-(-k2 // n_steps), BF16_SUBLANES)
    n_side_blocks = k2 // side_rows
    assert n_side_blocks * side_rows == k2 and n_side_blocks <= n_steps

    def side_block(jj, i):
        return jnp.minimum(jnp.where(jj == 0, 0, (jj - 1) * n_row_tiles + i), n_side_blocks - 1)

    w_spec = pl.BlockSpec((None, chunk, MM_COLS),
                          lambda jj, i: (layer, i, jnp.minimum(jj, n_col_tiles - 1)))
    def row_tile(jj, i):
        return jnp.where(jj == 0, 0, i)

    out_spec = pl.BlockSpec((rows, MM_COLS),
                            lambda jj, i: (row_tile(jj, i), jnp.maximum(jj - 1, 0)))
    return pl.pallas_call(
        functools.partial(_streamed_matmul_kernel, n_weights=len(weights), epilogue=epilogue),
        grid=(n_col_tiles + 1, n_row_tiles),
        in_specs=([pl.BlockSpec((rows, k), lambda jj, i: (row_tile(jj, i), 0)),
                   pl.BlockSpec((rows, 1), lambda jj, i: (row_tile(jj, i), 0))]
                  + [w_spec] * len(weights)
                  + [pl.BlockSpec((None, side_rows, n2),
                                  lambda jj, i: (layer, side_block(jj, i), 0))]),
        out_specs=[out_spec,
                   pl.BlockSpec((side_rows, n2), lambda jj, i: (side_block(jj, i), 0))],
        out_shape=[jax.ShapeDtypeStruct((m, n), out_dtype),
                   jax.ShapeDtypeStruct((k2, n2), BF16)],
        scratch_shapes=[pltpu.VMEM((2, k, MM_COLS), BF16) for _ in weights],
        compiler_params=_params("arbitrary", "arbitrary"),
        name=name,
    )(xg, r, *weights, side)


def _swiglu_gate(g, u):
    h = 0.5 * g
    return (h + h * jnp.tanh(h)) * u


def _gelu_tanh(x):
    return 0.5 * x * (1.0 + jnp.tanh(0.7978845608028654 * (x + 0.044715 * (x * x * x))))


def _ffn_up(xg, r, wg, wu, wd, layer):
    return _streamed_matmul(xg, r, [wg, wu], wd, layer, _swiglu_gate, BF16, "ffn_up")


def _proj(xg, r, w, w_next, layer):
    return _streamed_matmul(xg, r, [w], w_next, layer, lambda acc: acc, BF16, "proj")


def _residual_matmul_kernel(a_ref, w_ref, x_ref, *rest, scale, emit_norm):
    if emit_norm:
        gn_ref, xo_ref, xg_ref, r_ref, ssq_s = rest
        j = pl.program_id(1)

        @pl.when(j == 0)
        def _():
            ssq_s[...] = jnp.zeros_like(ssq_s)
    else:
        (xo_ref,) = rest

    sub = a_ref.shape[0] // ROW_SPLITS
    for h in range(ROW_SPLITS):
        rows = slice(h * sub, (h + 1) * sub)
        acc = jnp.dot(a_ref[rows, :], w_ref[...], preferred_element_type=F32)
        xn = x_ref[rows, :] + scale * acc
        xo_ref[rows, :] = xn
        if emit_norm:
            xg_ref[rows, :] = (xn * gn_ref[...]).astype(xg_ref.dtype)
            ssq_s[rows, :] += jnp.sum(xn * xn, axis=-1, keepdims=True)

    if emit_norm:
        @pl.when(j == pl.num_programs(1) - 1)
        def _():
            r_ref[...] = lax.rsqrt(ssq_s[...] * (1.0 / D_MODEL) + NORM_EPS)


def _residual_matmul(a, w, x, scale, next_gain, in_place=True):
    m, k = a.shape
    n = w.shape[1]
    emit_norm = next_gain is not None
    tile_spec = pl.BlockSpec((MM_ROWS, MM_COLS), lambda i, j: (i, j))
    operands = [a, w, x]
    in_specs = [pl.BlockSpec((MM_ROWS, k), lambda i, j: (i, 0)),
                pl.BlockSpec((k, MM_COLS), lambda i, j: (0, j)),
                tile_spec]
    out_specs = [tile_spec]
    out_shape = [jax.ShapeDtypeStruct((m, n), F32)]
    scratch = []
    if emit_norm:
        assert n == D_MODEL
        operands.append(next_gain.reshape(1, n))
        in_specs.append(pl.BlockSpec((1, MM_COLS), lambda i, j: (0, j)))
        out_specs += [tile_spec, pl.BlockSpec((MM_ROWS, 1), lambda i, j: (i, 0))]
        out_shape += [jax.ShapeDtypeStruct((m, n), BF16),
                      jax.ShapeDtypeStruct((m, 1), F32)]
        scratch = [pltpu.VMEM((MM_ROWS, 1), F32)]
    outs = pl.pallas_call(
        functools.partial(_residual_matmul_kernel, scale=scale, emit_norm=emit_norm),
        grid=(m // MM_ROWS, n // MM_COLS),
        in_specs=in_specs,
        out_specs=out_specs,
        out_shape=out_shape,
        scratch_shapes=scratch,
        input_output_aliases={2: 0} if in_place else {},
        compiler_params=_params("parallel", "arbitrary"),
        name="residual_matmul",
    )(*operands)
    return outs if emit_norm else (outs[0], None, None)


def _time_chunks(ref, halo, n_time, chunk_fn):
    cols = ref.shape[-1]
    rows0 = jnp.concatenate([jnp.zeros((halo, cols), F32),
                             ref[0, 0:MIX_TIME, :].astype(F32)], axis=0)
    chunk_fn(0, rows0, True)

    def body(c, carry):
        t0 = pl.multiple_of(c * MIX_TIME, MIX_TIME)
        start = pl.multiple_of(t0 - halo, halo)
        chunk_fn(t0, ref[0, pl.ds(start, MIX_TIME + halo), :].astype(F32), False)
        return carry

    lax.fori_loop(1, n_time // MIX_TIME, body, 0)


def _pool_block(a_ref, pw_ref, ps_ref, o_ref, window):
    n_time = a_ref.shape[1]
    w = pw_ref[0]
    scale = ps_ref[...]

    def chunk(t0, rows, first):
        s = rows
        shift = 1
        while shift < window:
            s = s + pltpu.roll(s, shift, axis=0)
            shift *= 2
        if first:
            t = lax.broadcasted_iota(jnp.int32, (MIX_TIME, 1), 0)
            pooled = s[POOL_HALO:] / jnp.minimum(t + 1, window).astype(F32)
        else:
            pooled = s[POOL_HALO:] * (1.0 / window)
        pa = pooled - rows[POOL_HALO:]
        y = jnp.dot(pa.astype(BF16), w, preferred_element_type=F32) * scale
        o_ref[0, pl.ds(t0, MIX_TIME), :] = y.astype(o_ref.dtype)

    _time_chunks(a_ref, POOL_HALO, n_time, chunk)


def _conv_block(gb_ref, gc_ref, xc_ref, cw_ref, o_ref):
    n_time = gb_ref.shape[1]
    w0 = cw_ref[0:1, :]
    w1 = cw_ref[1:2, :]
    w2 = cw_ref[2:3, :]
    cols = gb_ref.shape[-1]

    def chunk_rows(ref, t0):
        start = pl.multiple_of(t0 - CONV_HALO, CONV_HALO)
        return ref[0, pl.ds(start, MIX_TIME + CONV_HALO), :].astype(F32)

    def compute(t0, z):
        conv = w2 * z + w1 * pltpu.roll(z, 1, axis=0) + w0 * pltpu.roll(z, 2, axis=0)
        y = gb_ref[0, pl.ds(t0, MIX_TIME), :].astype(F32) * conv[CONV_HALO:]
        o_ref[0, pl.ds(t0, MIX_TIME), :] = y.astype(o_ref.dtype)

    z_first = gc_ref[0, 0:MIX_TIME, :].astype(F32) * xc_ref[0, 0:MIX_TIME, :].astype(F32)
    compute(0, jnp.concatenate([jnp.zeros((CONV_HALO, cols), F32), z_first], axis=0))

    def body(c, carry):
        t0 = pl.multiple_of(c * MIX_TIME, MIX_TIME)
        compute(t0, chunk_rows(gc_ref, t0) * chunk_rows(xc_ref, t0))
        return carry

    lax.fori_loop(1, n_time // MIX_TIME, body, 0)


def _sgu_block(u_ref, v_ref, lg_ref, ws_ref, bs_ref, o_ref):
    n_time = u_ref.shape[1]
    row = lax.broadcasted_iota(jnp.int32, (SGU_CHUNK, SGU_CHUNK), 0)
    col = lax.broadcasted_iota(jnp.int32, (SGU_CHUNK, SGU_CHUNK), 1)
    causal = row >= col
    heads = []
    for hh in range(SGU_HEADS_PER_BLOCK):
        lanes = slice(hh * SGU_HEAD_DIM, (hh + 1) * SGU_HEAD_DIM)
        ws = jnp.where(causal, ws_ref[hh], 0.0).astype(BF16)
        heads.append((lanes, ws, bs_ref[hh], lg_ref[:, lanes]))

    def body(n, carry):
        t0 = pl.multiple_of(n * SGU_CHUNK, SGU_CHUNK)
        for lanes, ws, bias, gain in heads:
            v = _gelu_tanh(v_ref[0, pl.ds(t0, SGU_CHUNK), lanes].astype(F32))
            mu = jnp.mean(v, axis=-1, keepdims=True)
            vc = v - mu
            var = jnp.mean(vc * vc, axis=-1, keepdims=True)
            vn = vc * lax.rsqrt(var + LN_EPS) * gain
            mixed = jnp.dot(ws, vn.astype(BF16), preferred_element_type=F32) + bias
            u = _gelu_tanh(u_ref[0, pl.ds(t0, SGU_CHUNK), lanes].astype(F32))
            o_ref[0, pl.ds(t0, SGU_CHUNK), lanes] = (u * mixed).astype(o_ref.dtype)
        return carry

    lax.fori_loop(0, n_time // SGU_CHUNK, body, 0, unroll=SGU_UNROLL)


def _mixer_kernel(a_ref, gb_ref, gc_ref, xc_ref, u_ref, v_ref,
                  pw_ref, ps_ref, cw_ref, lg_ref, ws_ref, bs_ref, o_ref):
    j = pl.program_id(1)
    for g, window in enumerate(POOL_WINDOWS):
        @pl.when(j == g)
        def _(window=window):
            _pool_block(a_ref, pw_ref, ps_ref, o_ref, window)

    @pl.when(jnp.logical_and(j >= N_POOL_BLOCKS, j < N_POOL_BLOCKS + N_CONV_BLOCKS))
    def _():
        _conv_block(gb_ref, gc_ref, xc_ref, cw_ref, o_ref)

    @pl.when(j >= N_POOL_BLOCKS + N_CONV_BLOCKS)
    def _():
        _sgu_block(u_ref, v_ref, lg_ref, ws_ref, bs_ref, o_ref)


def _clamp(j, lo, n):
    return jnp.clip(j - lo, 0, n - 1)


def _mixers(proj, pool_w, pool_scale, conv_w, sgu_norm, sgu_w, sgu_b, layer):
    b, t, _ = proj.shape
    conv_lo = N_POOL_BLOCKS
    sgu_lo = N_POOL_BLOCKS + N_CONV_BLOCKS
    off_gb = POOL_WIDTH // MIX_COLS
    off_gc = off_gb + N_CONV_BLOCKS
    off_xc = off_gc + N_CONV_BLOCKS
    off_u = off_xc + N_CONV_BLOCKS
    off_v = off_u + N_SGU_BLOCKS

    def proj_spec(offset, lo, n):
        return pl.BlockSpec((1, t, MIX_COLS), lambda bi, j: (bi, 0, offset + _clamp(j, lo, n)))

    in_specs = [
        proj_spec(0, 0, N_POOL_BLOCKS),
        proj_spec(off_gb, conv_lo, N_CONV_BLOCKS),
        proj_spec(off_gc, conv_lo, N_CONV_BLOCKS),
        proj_spec(off_xc, conv_lo, N_CONV_BLOCKS),
        proj_spec(off_u, sgu_lo, N_SGU_BLOCKS),
        proj_spec(off_v, sgu_lo, N_SGU_BLOCKS),
        pl.BlockSpec((None, 1, POOL_GROUP_DIM, POOL_GROUP_DIM),
                     lambda bi, j: (layer, _clamp(j, 0, N_POOL_BLOCKS), 0, 0)),
        pl.BlockSpec((None, 1, MIX_COLS),
                     lambda bi, j: (layer, 0, _clamp(j, 0, N_POOL_BLOCKS))),
        pl.BlockSpec((None, CONV_K, MIX_COLS),
                     lambda bi, j: (layer, 0, _clamp(j, conv_lo, N_CONV_BLOCKS))),
        pl.BlockSpec((None, 1, MIX_COLS),
                     lambda bi, j: (layer, 0, _clamp(j, sgu_lo, N_SGU_BLOCKS))),
        pl.BlockSpec((None, SGU_HEADS_PER_BLOCK, SGU_CHUNK, SGU_CHUNK),
                     lambda bi, j: (layer, _clamp(j, sgu_lo, N_SGU_BLOCKS), 0, 0)),
        pl.BlockSpec((None, SGU_HEADS_PER_BLOCK, SGU_CHUNK, 1),
                     lambda bi, j: (layer, _clamp(j, sgu_lo, N_SGU_BLOCKS), 0, 0)),
    ]
    return pl.pallas_call(
        _mixer_kernel,
        grid=(b, N_MIX_BLOCKS),
        in_specs=in_specs,
        out_specs=pl.BlockSpec((1, t, MIX_COLS), lambda bi, j: (bi, 0, j)),
        out_shape=jax.ShapeDtypeStruct((b, t, D_MODEL), BF16),
        compiler_params=_params("parallel", "arbitrary"),
        name="mixers",
    )(proj, proj, proj, proj, proj, proj,
      pool_w, pool_scale, conv_w, sgu_norm, sgu_w, sgu_b)


def kernel(x, ffn1_norm, ffn1_gate, ffn1_up, ffn1_down, mix_norm, w_in, pool_w, pool_scale,
           conv_w, sgu_norm, sgu_w, sgu_b, w_out, ffn2_norm, ffn2_gate, ffn2_up, ffn2_down,
           final_norm):
    b, t, d = x.shape
    m = b * t
    x = x.reshape(m, d)
    pool_w = pool_w.astype(BF16)
    pool_scale = pool_scale[:, None, :]
    sgu_norm = sgu_norm[:, None, :]
    sgu_b = sgu_b[..., None]

    xg, r = _norm_split(x, ffn1_norm[0])
    for l in range(DEPTH):
        act, wd = _ffn_up(xg, r, ffn1_gate, ffn1_up, ffn1_down, l)
        x, xg, r = _residual_matmul(act, wd, x, 0.5, mix_norm[l], in_place=l > 0)

        proj, wo = _proj(xg, r, w_in, w_out, l)
        y = _mixers(proj.reshape(b, t, PROJ_WIDTH), pool_w, pool_scale, conv_w,
                    sgu_norm, sgu_w, sgu_b, l)
        x, xg, r = _residual_matmul(y.reshape(m, d), wo, x, 1.0, ffn2_norm[l])

        act, wd = _ffn_up(xg, r, ffn2_gate, ffn2_up, ffn2_down, l)
        next_gain = ffn1_norm[l + 1] if l + 1 < DEPTH else None
        x, xg, r = _residual_matmul(act, wd, x, 0.5, next_gain)
    return _rmsnorm(x, final_norm).reshape(b, t, d)
```
